```python
import math
import numpy as np
import jax
import jax.numpy as jnp
from jax import lax

D_MODEL = 2048
BATCH = 2
SEQ = 4096
DEPTH = 4
DEC_BATCH = 8
DEC_SEQ = 4
PAST_LEN = 16384
PAGE_SIZE = 128

BRANCH_WIDTH = D_MODEL // 2
HEAD_DIM = 128
N_HEADS = BRANCH_WIDTH // HEAD_DIM
GQA_REP = 4
N_KV_HEADS = N_HEADS // GQA_REP
ROPE_THETA = 10000.0
CMP_LEN = 32
CMP_STRIDE = 16
SLC_BLK = 64
N_SEL = 16
WINDOW = 512
Q_BLOCK = 128
SEL_BONUS = 1.0e4
POOL_WINDOWS = (2, 4, 8, 16)
N_POOL_GROUPS = len(POOL_WINDOWS)
POOL_GROUP = BRANCH_WIDTH // N_POOL_GROUPS
POOL_STATE = max(POOL_WINDOWS) - 1
M_HEAD_DIM = 64
M_HEADS = BRANCH_WIDTH // M_HEAD_DIM
M_STATE = 128
M_GROUPS = 2
CONV_W = 4
CONV_CH = BRANCH_WIDTH + 2 * M_GROUPS * M_STATE
SSD_CHUNK = 128
EPS = 1e-6
IN_WIDTHS = (N_HEADS * HEAD_DIM, 6 * N_KV_HEADS * HEAD_DIM, 3 * N_HEADS, BRANCH_WIDTH, BRANCH_WIDTH,
             BRANCH_WIDTH, BRANCH_WIDTH, CONV_CH, M_HEADS, 3 * D_MODEL)
N_IN = sum(IN_WIDTHS)

kernel_name = 'nsa_pool_ssd_parallel_hybrid_step'


def _rmsnorm(x, g):
    x32 = x.astype(jnp.float32)
    r = x32 * lax.rsqrt(jnp.mean(x32 * x32, axis=-1, keepdims=True) + EPS)
    return (r * g.astype(jnp.float32)).astype(x.dtype)


def _split(z, widths):
    return jnp.split(z, np.cumsum(widths)[:-1].tolist(), axis=-1)


def _rope_tables(pos0, length):
    inv = ROPE_THETA ** (-jnp.arange(0, HEAD_DIM, 2, dtype=jnp.float32) / HEAD_DIM)
    ang = (pos0 + jnp.arange(length, dtype=jnp.float32))[:, None] * inv[None, :]
    return jnp.cos(ang), jnp.sin(ang)


def _rope(x, cos, sin):
    x32 = x.astype(jnp.float32)
    x1, x2 = jnp.split(x32, 2, axis=-1)
    return jnp.concatenate([x1 * cos - x2 * sin, x2 * cos + x1 * sin], axis=-1).astype(x.dtype)


def _masked_softmax(s, mask):
    s = jnp.where(mask, s.astype(jnp.float32), -jnp.inf)
    m = jnp.max(s, axis=-1, keepdims=True)
    m = jnp.where(jnp.isfinite(m), m, 0.0)
    e = jnp.exp(s - m)
    return e / jnp.maximum(jnp.sum(e, axis=-1, keepdims=True), 1e-30)


def _nsa(q, kv_full, win_all, gates, q_pos0, cmp_pe, cmp_w):
    bsz, tq = q.shape[0], q.shape[1]
    tk = kv_full.shape[1]
    scale = HEAD_DIM ** -0.5
    n_chunks = tk // CMP_STRIDE
    r = CMP_LEN // CMP_STRIDE
    n_cmp = n_chunks - r + 1
    chunks = kv_full[:, :n_chunks * CMP_STRIDE, :2].reshape(bsz, n_chunks, CMP_STRIDE, 2, N_KV_HEADS, HEAD_DIM)
    blocks = jnp.concatenate([chunks[:, j:j + n_cmp] for j in range(r)], axis=2)
    blocks = blocks + jnp.transpose(cmp_pe, (1, 0, 2))[:, :, None, :]
    kv_c = jnp.einsum('bilcgd,clde->bicge', blocks, cmp_w.reshape(2, CMP_LEN, HEAD_DIM, HEAD_DIM))
    k_c, v_c = kv_c[:, :, 0], kv_c[:, :, 1]
    cmp_start = jnp.arange(n_cmp) * CMP_STRIDE
    cmp_end = cmp_start + CMP_LEN - 1
    n_slc = -(-tk // SLC_BLK)
    pad = n_slc * SLC_BLK - tk
    kv_s = jnp.pad(kv_full[:, :, 2:4], ((0, 0), (0, pad), (0, 0), (0, 0), (0, 0)))
    kv_s = kv_s.reshape(bsz, n_slc, SLC_BLK, 2, N_KV_HEADS, HEAD_DIM).transpose(3, 0, 4, 1, 2, 5)
    k_s, v_s = kv_s[0], kv_s[1]
    slc_start = jnp.arange(n_slc) * SLC_BLK
    overlap = ((cmp_start[:, None] < slc_start[None, :] + SLC_BLK)
               & (cmp_start[:, None] + CMP_LEN > slc_start[None, :])).astype(jnp.float32)
    n_sel = min(N_SEL, n_slc)
    qg = q.reshape(bsz, tq, N_KV_HEADS, GQA_REP, HEAD_DIM)
    gg = gates.reshape(bsz, tq, N_KV_HEADS, GQA_REP, 3)
    qblk = Q_BLOCK if tq % Q_BLOCK == 0 else tq
    bi = jnp.arange(bsz)[:, None, None]
    gi = jnp.arange(N_KV_HEADS)[None, :, None]
    blk_ids = jnp.arange(n_slc)

    def one_block(i):
        s0 = i * qblk
        qb = lax.dynamic_slice_in_dim(qg, s0, qblk, axis=1)
        gb = lax.dynamic_slice_in_dim(gg, s0, qblk, axis=1)
        t = q_pos0 + s0 + jnp.arange(qblk)
        sc = jnp.einsum('bqgrd,bigd->bqgri', qb, k_c) * scale
        pc = _masked_softmax(sc, (cmp_end[None, :] <= t[:, None])[None, :, None, None, :])
        o_cmp = jnp.einsum('bqgri,bigd->bqgrd', pc, v_c)
        imp = jnp.einsum('bqgri,ij->bqgj', pc, overlap)
        cur = t // SLC_BLK
        forced = (blk_ids[None, :] == 0) | (blk_ids[None, :] == cur[:, None]) | (blk_ids[None, :] == cur[:, None] - 1)
        valid = slc_start[None, :] <= t[:, None]
        score = jnp.where(valid[None, :, None, :],
                          imp + SEL_BONUS * forced[None, :, None, :].astype(jnp.float32), -jnp.inf)
        _, idx = lax.top_k(score, n_sel)
        idx_flat = idx.transpose(0, 2, 1, 3).reshape(bsz, N_KV_HEADS, qblk * n_sel)
        kg = k_s[bi, gi, idx_flat].reshape(bsz, N_KV_HEADS, qblk, n_sel, SLC_BLK, HEAD_DIM)
        vg = v_s[bi, gi, idx_flat].reshape(bsz, N_KV_HEADS, qblk, n_sel, SLC_BLK, HEAD_DIM)
        ss = jnp.einsum('bqgrd,bgqnkd->bqgrnk', qb, kg) * scale
        kpos = idx[..., None] * SLC_BLK + jnp.arange(SLC_BLK)
        smask = (kpos <= t[None, :, None, None, None]).reshape(bsz, qblk, N_KV_HEADS, 1, n_sel * SLC_BLK)
        ps = _masked_softmax(ss.reshape(bsz, qblk, N_KV_HEADS, GQA_REP, n_sel * SLC_BLK), smask)
        o_slc = jnp.einsum('bqgrnk,bgqnkd->bqgrd',
                           ps.reshape(bsz, qblk, N_KV_HEADS, GQA_REP, n_sel, SLC_BLK), vg)
        wb = lax.dynamic_slice_in_dim(win_all, s0, WINDOW + qblk, axis=1)
        wpos = q_pos0 - WINDOW + s0 + jnp.arange(WINDOW + qblk)
        dpos = t[:, None] - wpos[None, :]
        wmask = (wpos[None, :] >= 0) & (dpos >= 0) & (dpos < WINDOW)
        sw = jnp.einsum('bqgrd,bkgd->bqgrk', qb, wb[:, :, 0]) * scale
        pw = _masked_softmax(sw, wmask[None, :, None, None, :])
        o_win = jnp.einsum('bqgrk,bkgd->bqgrd', pw, wb[:, :, 1])
        return gb[..., 0:1] * o_cmp + gb[..., 1:2] * o_slc + gb[..., 2:3] * o_win

    out = lax.map(one_block, jnp.arange(tq // qblk))
    return jnp.moveaxis(out, 0, 1).reshape(bsz, tq, N_HEADS * HEAD_DIM).astype(q.dtype)


def _pool_mix(u, prefix, pos0, pool_w, pool_scale):
    bsz, T, C = u.shape
    p = prefix.shape[1]
    up = jnp.concatenate([prefix, u], axis=1)
    up32 = up.astype(jnp.float32)
    cs = jnp.concatenate([jnp.zeros((bsz, 1, C), jnp.float32), jnp.cumsum(up32, axis=1)], axis=1)
    pos = pos0 + jnp.arange(T)
    means = []
    for gidx, w in enumerate(POOL_WINDOWS):
        c0, c1 = gidx * POOL_GROUP, (gidx + 1) * POOL_GROUP
        hi = cs[:, p + 1:p + 1 + T, c0:c1]
        lo = cs[:, p + 1 - w:p + 1 - w + T, c0:c1]
        cnt = jnp.minimum(w, pos + 1).astype(jnp.float32)[None, :, None]
        means.append((hi - lo) / cnt)
    pooled = jnp.concatenate(means, axis=-1) - up32[:, p:]
    mixed = jnp.einsum('btgc,gcd->btgd', pooled.reshape(bsz, T, N_POOL_GROUPS, POOL_GROUP), pool_w)
    mixed = mixed.reshape(bsz, T, C) * pool_scale
    return mixed.astype(u.dtype), up[:, -POOL_STATE:]


def _segsum(a):
    n = a.shape[-1]
    cs = jnp.cumsum(a, axis=-1)
    d = cs[..., :, None] - cs[..., None, :]
    return jnp.where(jnp.tril(jnp.ones((n, n), dtype=bool)), d, -jnp.inf)


def _ssd(x, dt, a_head, bh, ch, init):
    bsz, length = x.shape[0], x.shape[1]
    cl = min(SSD_CHUNK, length)
    pad = (-length) % cl
    if pad:
        padw = lambda v: jnp.pad(v, [(0, 0), (0, pad)] + [(0, 0)] * (v.ndim - 2))
        x, dt, bh, ch = padw(x), padw(dt), padw(bh), padw(ch)
    nc = (length + pad) // cl
    xdt = (x * dt[..., None]).reshape(bsz, nc, cl, M_HEADS, M_HEAD_DIM)
    bc = bh.reshape(bsz, nc, cl, M_HEADS, M_STATE)
    cc = ch.reshape(bsz, nc, cl, M_HEADS, M_STATE)
    la = jnp.transpose((dt * a_head).reshape(bsz, nc, cl, M_HEADS), (0, 3, 1, 2))
    acs = jnp.cumsum(la, axis=-1)
    decay_in = jnp.exp(_segsum(la))
    y_diag = jnp.einsum('bclhn,bcshn,bhcls,bcshp->bclhp', cc, bc, decay_in, xdt)
    decay_to_end = jnp.exp(acs[..., -1:] - acs)
    states = jnp.einsum('bclhn,bhcl,bclhp->bchpn', bc, decay_to_end, xdt)
    states = jnp.concatenate([init[:, None], states], axis=1)
    decay_chunk = jnp.exp(_segsum(jnp.pad(acs[..., -1], ((0, 0), (0, 0), (1, 0)))))
    states = jnp.einsum('bhzc,bchpn->bzhpn', decay_chunk, states)
    prev, final = states[:, :-1], states[:, -1]
    y_off = jnp.einsum('bclhn,bchpn,bhcl->bclhp', cc, prev, jnp.exp(acs))
    y = (y_diag + y_off).reshape(bsz, nc * cl, M_HEADS, M_HEAD_DIM)[:, :length]
    return y, final


def _layer(x, pos0, past, norm_w, w_in, qk_gain, cmp_pe, cmp_w, pool_w, pool_scale, conv_w, conv_b,
           dt_bias, a_log, d_skip, mnorm_w, w_branch, w_out):
    f32 = jnp.float32
    bsz, T, _ = x.shape
    h = _rmsnorm(x, norm_w)
    zin = h @ w_in
    q, kv, nsa_g, nsa_z, pool_u, pool_z, m_z, xbc, m_dt, merge_g = _split(zin, IN_WIDTHS)
    cos, sin = _rope_tables(pos0, T)
    q = _rope(_rmsnorm(q.reshape(bsz, T, N_HEADS, HEAD_DIM), qk_gain[0]), cos[:, None, :], sin[:, None, :])
    kv = kv.reshape(bsz, T, 3, 2, N_KV_HEADS, HEAD_DIM)
    k = _rope(_rmsnorm(kv[:, :, :, 0], qk_gain[1:, None, :]), cos[:, None, None, :], sin[:, None, None, :])
    kv = jnp.stack([k, kv[:, :, :, 1]], axis=3)
    rows = kv[:, :, :2].reshape(bsz, T, 4, N_KV_HEADS, HEAD_DIM)
    win_rows = kv[:, :, 2]
    if past is None:
        kv_full = rows
        win_cat = win_rows
        n_past_win = 0
    else:
        kv_full = jnp.concatenate([past['kv'], rows], axis=1)
        win_cat = jnp.concatenate([past['win'], win_rows], axis=1)
        n_past_win = past['win'].shape[1]
    win_all = jnp.pad(win_cat, ((0, 0), (WINDOW - n_past_win, 0), (0, 0), (0, 0), (0, 0)))
    new_win = win_cat[:, -min(WINDOW, n_past_win + T):]
    gates = jax.nn.sigmoid(nsa_g.astype(f32)).reshape(bsz, T, N_HEADS, 3)
    nsa_out = _nsa(q, kv_full, win_all, gates, pos0, cmp_pe, cmp_w) * jax.nn.silu(nsa_z)
    pool_prefix = jnp.zeros((bsz, POOL_STATE, BRANCH_WIDTH), x.dtype) if past is None else past['pool']
    pool_out, new_pool = _pool_mix(pool_u, pool_prefix, pos0, pool_w, pool_scale)
    pool_out = pool_out * jax.nn.silu(pool_z)
    conv_prefix = jnp.zeros((bsz, CONV_W - 1, CONV_CH), x.dtype) if past is None else past['conv']
    xbc_p = jnp.concatenate([conv_prefix, xbc], axis=1)
    new_conv = xbc_p[:, -(CONV_W - 1):]
    acc = conv_b
    for j in range(CONV_W):
        acc = acc + xbc_p[:, j:j + T] * conv_w[j]
    xbc_c = jax.nn.silu(acc)
    xs, bm, cm = _split(xbc_c, (BRANCH_WIDTH, M_GROUPS * M_STATE, M_GROUPS * M_STATE))
    xs = xs.reshape(bsz, T, M_HEADS, M_HEAD_DIM).astype(f32)
    rep = M_HEADS // M_GROUPS
    bh = jnp.repeat(bm.reshape(bsz, T, M_GROUPS, M_STATE), rep, axis=2).astype(f32)
    ch = jnp.repeat(cm.reshape(bsz, T, M_GROUPS, M_STATE), rep, axis=2).astype(f32)
    dt = jax.nn.softplus(m_dt.astype(f32) + dt_bias.astype(f32))
    a_head = -jnp.exp(a_log.astype(f32))
    init = jnp.zeros((bsz, M_HEADS, M_HEAD_DIM, M_STATE), f32) if past is None else past['ssm'].astype(f32)
    y, ssm_final = _ssd(xs, dt, a_head, bh, ch, init)
    y = (y + d_skip.astype(f32)[:, None] * xs).reshape(bsz, T, BRANCH_WIDTH) * jax.nn.silu(m_z.astype(f32))
    yg = y.reshape(bsz, T, M_GROUPS, BRANCH_WIDTH // M_GROUPS)
    yg = yg * lax.rsqrt(jnp.mean(yg * yg, axis=-1, keepdims=True) + EPS)
    m_out = (yg.reshape(bsz, T, BRANCH_WIDTH) * mnorm_w.astype(f32)).astype(x.dtype)
    br = jnp.stack([nsa_out, pool_out, m_out], axis=2)
    proj = jnp.einsum('btkw,kwd->btkd', br, w_branch)
    g = jax.nn.sigmoid(merge_g.astype(f32)).reshape(bsz, T, 3, D_MODEL)
    merged = jnp.sum(g * proj, axis=2).astype(x.dtype)
    y_out = x + merged @ w_out
    return y_out, rows, new_win, new_pool, new_conv, ssm_final.astype(x.dtype)


def setup_inputs(seed: int = 0) -> dict:
    key = jax.random.key(seed)
    ks = jax.random.split(key, 24)
    f32 = jnp.float32
    n_pages = PAST_LEN // PAGE_SIZE
    n_pool = (DEC_BATCH * n_pages * 5) // 4
    wb = min(WINDOW, PAST_LEN)
    nrm = lambda k, shape, s: s * jax.random.normal(k, shape, f32)
    x_prompt = nrm(ks[0], (BATCH, SEQ, D_MODEL), 1.0)
    x_sample = nrm(ks[1], (DEC_BATCH, DEC_SEQ, D_MODEL), 1.0)
    cache_kv = nrm(ks[2], (DEPTH, n_pool, PAGE_SIZE, 4, N_KV_HEADS, HEAD_DIM), 1.0)
    cache_win = nrm(ks[3], (DEPTH, DEC_BATCH, wb, 2, N_KV_HEADS, HEAD_DIM), 1.0)
    state_pool = nrm(ks[4], (DEPTH, DEC_BATCH, POOL_STATE, BRANCH_WIDTH), 1.0)
    state_conv = nrm(ks[5], (DEPTH, DEC_BATCH, CONV_W - 1, CONV_CH), 1.0)
    state_ssm = nrm(ks[6], (DEPTH, DEC_BATCH, M_HEADS, M_HEAD_DIM, M_STATE), 0.5)
    page_table = jax.random.permutation(ks[7], n_pool)[:DEC_BATCH * n_pages].reshape(DEC_BATCH, n_pages).astype(jnp.int32)
    norm_w = 1.0 + nrm(ks[8], (DEPTH, D_MODEL), 0.05)
    w_in = nrm(ks[9], (DEPTH, D_MODEL, N_IN), D_MODEL ** -0.5)
    qk_gain = 1.0 + nrm(ks[10], (DEPTH, 4, HEAD_DIM), 0.05)
    cmp_pe = nrm(ks[11], (DEPTH, 2, CMP_LEN, HEAD_DIM), 0.1)
    cmp_w = nrm(ks[12], (DEPTH, 2, CMP_LEN * HEAD_DIM, HEAD_DIM), (CMP_LEN * HEAD_DIM) ** -0.5)
    pool_w = nrm(ks[13], (DEPTH, N_POOL_GROUPS, POOL_GROUP, POOL_GROUP), POOL_GROUP ** -0.5)
    pool_scale = 1.0 + nrm(ks[14], (DEPTH, BRANCH_WIDTH), 0.1)
    conv_w = nrm(ks[15], (DEPTH, CONV_W, CONV_CH), CONV_W ** -0.5)
    conv_b = nrm(ks[16], (DEPTH, CONV_CH), 0.02)
    dt0 = jnp.exp(jax.random.uniform(ks[17], (DEPTH, M_HEADS), f32, math.log(1e-3), math.log(1e-1)))
    dt_bias = dt0 + jnp.log(-jnp.expm1(-dt0))
    a_log = jnp.log(jax.random.uniform(ks[18], (DEPTH, M_HEADS), f32, 1.0, 16.0))
    d_skip = 1.0 + nrm(ks[19], (DEPTH, M_HEADS), 0.1)
    mnorm_w = 1.0 + nrm(ks[20], (DEPTH, BRANCH_WIDTH), 0.05)
    w_branch = nrm(ks[21], (DEPTH, 3, BRANCH_WIDTH, D_MODEL), BRANCH_WIDTH ** -0.5)
    w_out = nrm(ks[22], (DEPTH, D_MODEL, D_MODEL), D_MODEL ** -0.5)
    return {'x_prompt': x_prompt, 'x_sample': x_sample, 'cache_kv': cache_kv, 'cache_win': cache_win,
            'state_pool': state_pool, 'state_conv': state_conv, 'state_ssm': state_ssm,
            'page_table': page_table, 'norm_w': norm_w, 'w_in': w_in, 'qk_gain': qk_gain,
            'cmp_pe': cmp_pe, 'cmp_w': cmp_w, 'pool_w': pool_w, 'pool_scale': pool_scale,
            'conv_w': conv_w, 'conv_b': conv_b, 'dt_bias': dt_bias, 'a_log': a_log, 'd_skip': d_skip,
            'mnorm_w': mnorm_w, 'w_branch': w_branch, 'w_out': w_out}


def reference(x_prompt, x_sample, cache_kv, cache_win, state_pool, state_conv, state_ssm, page_table,
              norm_w, w_in, qk_gain, cmp_pe, cmp_w, pool_w, pool_scale, conv_w, conv_b, dt_bias, a_log,
              d_skip, mnorm_w, w_branch, w_out):
    dec_b, n_pages = page_table.shape
    past_len = n_pages * cache_kv.shape[2]
    xp, xs = x_prompt, x_sample
    outs_p = [[] for _ in range(5)]
    outs_s = [[] for _ in range(5)]
    for l in range(DEPTH):
        params = (norm_w[l], w_in[l], qk_gain[l], cmp_pe[l], cmp_w[l], pool_w[l], pool_scale[l], conv_w[l],
                  conv_b[l], dt_bias[l], a_log[l], d_skip[l], mnorm_w[l], w_branch[l], w_out[l])
        xp, *st_p = _layer(xp, 0, None, *params)
        past = {'kv': cache_kv[l][page_table].reshape(dec_b, past_len, 4, N_KV_HEADS, HEAD_DIM),
                'win': cache_win[l], 'pool': state_pool[l], 'conv': state_conv[l], 'ssm': state_ssm[l]}
        xs, *st_s = _layer(xs, past_len, past, *params)
        for lst, v in zip(outs_p, st_p):
            lst.append(v)
        for lst, v in zip(outs_s, st_s):
            lst.append(v)
    kv_p, win_p, pool_p, conv_p, ssm_p = [jnp.stack(v) for v in outs_p]
    kv_s, win_s, pool_s, conv_s, ssm_s = [jnp.stack(v) for v in outs_s]
    return (xp, xs, kv_p, win_p, pool_p, conv_p, ssm_p, kv_s, win_s, pool_s, conv_s, ssm_s)
```

```python
import functools

import numpy as np
import jax
import jax.numpy as jnp
from jax import lax
from jax.experimental import pallas as pl
from jax.experimental.pallas import tpu as pltpu

F32 = jnp.float32
BF16 = jnp.bfloat16

D_MODEL = 2048
BRANCH_WIDTH = D_MODEL // 2
HEAD_DIM = 128
N_HEADS = BRANCH_WIDTH // HEAD_DIM
GQA_REP = 4
N_KV_HEADS = N_HEADS // GQA_REP
ROPE_THETA = 10000.0
CMP_LEN = 32
CMP_STRIDE = 16
SLC_BLK = 64
N_SEL = 16
WINDOW = 512
Q_BLOCK = 128
SEL_BONUS = 1.0e4
POOL_WINDOWS = (2, 4, 8, 16)
POOL_GROUP = BRANCH_WIDTH // len(POOL_WINDOWS)
POOL_STATE = max(POOL_WINDOWS) - 1
M_HEAD_DIM = 64
M_HEADS = BRANCH_WIDTH // M_HEAD_DIM
M_STATE = 128
M_GROUPS = 2
CONV_W = 4
CONV_CH = BRANCH_WIDTH + 2 * M_GROUPS * M_STATE
SSD_CHUNK = 128
EPS = 1e-6
SCALE = HEAD_DIM ** -0.5
NEG_BIG = -1.0e30

C_Q = 0
C_NSAZ = 1024
C_POOLU = 2048
C_POOLZ = 3072
C_MZ = 4096
C_XS = 5120
C_MG = 6144
C_KV = 12288
C_B = 13824
C_C = 14080
N_MAIN = 14336
N_SMALL = 384

VMEM_LIMIT = 56 * 1024 * 1024
HI = lax.Precision.HIGHEST


def _cparams(sem):
    return pltpu.CompilerParams(dimension_semantics=sem, vmem_limit_bytes=VMEM_LIMIT)


def _nt(a, b):
    return lax.dot_general(a, b, (((1,), (1,)), ((), ())), preferred_element_type=F32)


def _silu(x):
    return x * jax.nn.sigmoid(x)


def _rmsnorm_kernel(x_ref, g_ref, o_ref):
    x = x_ref[...]
    ms = jnp.mean(x * x, axis=-1, keepdims=True)
    o_ref[...] = (x * lax.rsqrt(ms + EPS) * g_ref[...]).astype(o_ref.dtype)


def _rmsnorm(x, g, tr):
    m, d = x.shape
    return pl.pallas_call(
        _rmsnorm_kernel,
        grid=(m // tr,),
        in_specs=[pl.BlockSpec((tr, d), lambda i: (i, 0)), pl.BlockSpec((1, d), lambda i: (0, 0))],
        out_specs=pl.BlockSpec((tr, d), lambda i: (i, 0)),
        out_shape=jax.ShapeDtypeStruct((m, d), BF16),
        compiler_params=_cparams(("parallel",)),
        name="rmsnorm",
    )(x, g.reshape(1, d))


def _mm_kernel(a_ref, b_ref, o_ref):
    o_ref[...] = jnp.dot(a_ref[...], b_ref[...], preferred_element_type=F32).astype(o_ref.dtype)


def _matmul(a, b, tm, tn, name):
    m, k = a.shape
    n = b.shape[1]
    return pl.pallas_call(
        _mm_kernel,
        grid=(m // tm, n // tn),
        in_specs=[pl.BlockSpec((tm, k), lambda i, j: (i, 0)), pl.BlockSpec((k, tn), lambda i, j: (0, j))],
        out_specs=pl.BlockSpec((tm, tn), lambda i, j: (i, j)),
        out_shape=jax.ShapeDtypeStruct((m, n), F32),
        compiler_params=_cparams(("parallel", "parallel")),
        name=name,
    )(a, b)


def _merge_kernel(a_ref, p_ref, m_ref, w_ref, g0_ref, g1_ref, g2_ref, o_ref):
    acc = jax.nn.sigmoid(g0_ref[...]) * jnp.dot(a_ref[...], w_ref[0], preferred_element_type=F32)
    acc = acc + jax.nn.sigmoid(g1_ref[...]) * jnp.dot(p_ref[...], w_ref[1], preferred_element_type=F32)
    acc = acc + jax.nn.sigmoid(g2_ref[...]) * jnp.dot(m_ref[...], w_ref[2], preferred_element_type=F32)
    o_ref[...] = acc.astype(o_ref.dtype)


def _merge(nsa_out, pool_out, m_out, w_branch, z_main, tm, tn):
    m = nsa_out.shape[0]
    gblk = [(C_MG + b * D_MODEL) // tn for b in range(3)]
    act = pl.BlockSpec((tm, BRANCH_WIDTH), lambda i, j: (i, 0))
    return pl.pallas_call(
        _merge_kernel,
        grid=(m // tm, D_MODEL // tn),
        in_specs=[act, act, act,
                  pl.BlockSpec((3, BRANCH_WIDTH, tn), lambda i, j: (0, 0, j)),
                  pl.BlockSpec((tm, tn), lambda i, j: (i, gblk[0] + j)),
                  pl.BlockSpec((tm, tn), lambda i, j: (i, gblk[1] + j)),
                  pl.BlockSpec((tm, tn), lambda i, j: (i, gblk[2] + j))],
        out_specs=pl.BlockSpec((tm, tn), lambda i, j: (i, j)),
        out_shape=jax.ShapeDtypeStruct((m, D_MODEL), BF16),
        compiler_params=_cparams(("parallel", "parallel")),
        name="merge",
    )(nsa_out, pool_out, m_out, w_branch, z_main, z_main, z_main)


def _outproj_kernel(x_ref, a_ref, w_ref, o_ref):
    o_ref[...] = x_ref[...] + jnp.dot(a_ref[...], w_ref[...], preferred_element_type=F32)


def _outproj(x, merged, w_out, tm, tn):
    m = x.shape[0]
    return pl.pallas_call(
        _outproj_kernel,
        grid=(m // tm, D_MODEL // tn),
        in_specs=[pl.BlockSpec((tm, tn), lambda i, j: (i, j)),
                  pl.BlockSpec((tm, D_MODEL), lambda i, j: (i, 0)),
                  pl.BlockSpec((D_MODEL, tn), lambda i, j: (0, j))],
        out_specs=pl.BlockSpec((tm, tn), lambda i, j: (i, j)),
        out_shape=jax.ShapeDtypeStruct((m, D_MODEL), F32),
        compiler_params=_cparams(("parallel", "parallel")),
        name="outproj",
    )(x, merged, w_out)


def _prep_kernel(zq_ref, zkv_ref, cos_ref, sin_ref, gain_ref, q_ref, rows_ref, win_ref, cmpin_ref, kvatt_ref):
    cos = cos_ref[...]
    sin = sin_ref[...]

    def norm_rope(x, gi):
        ms = jnp.mean(x * x, axis=-1, keepdims=True)
        y = x * lax.rsqrt(ms + EPS) * gain_ref[gi:gi + 1, :]
        return y * cos + pltpu.roll(y, HEAD_DIM // 2, 1) * sin

    for h in range(N_HEADS):
        sl = slice(h * HEAD_DIM, (h + 1) * HEAD_DIM)
        q_ref[:, sl] = norm_rope(zq_ref[:, sl], 0).astype(q_ref.dtype)
    for br in range(3):
        for g in range(N_KV_HEADS):
            ck = br * 512 + g * HEAD_DIM
            cv = br * 512 + 256 + g * HEAD_DIM
            k = norm_rope(zkv_ref[:, ck:ck + HEAD_DIM], 1 + br)
            v = zkv_ref[:, cv:cv + HEAD_DIM]
            if br < 2:
                rows_ref[:, ck:ck + HEAD_DIM] = k
                rows_ref[:, cv:cv + HEAD_DIM] = v
            else:
                win_ref[:, g * HEAD_DIM:(g + 1) * HEAD_DIM] = k
                win_ref[:, 256 + g * HEAD_DIM:256 + (g + 1) * HEAD_DIM] = v
            if br == 0:
                cmpin_ref[g] = k
                cmpin_ref[2 + g] = v
            else:
                base = (br - 1) * 4
                kvatt_ref[base + g] = k.astype(kvatt_ref.dtype)
                kvatt_ref[base + 2 + g] = v.astype(kvatt_ref.dtype)


def _prep(z_main, cos2, sin2, gain8, tr):
    m = z_main.shape[0]
    return pl.pallas_call(
        _prep_kernel,
        grid=(m // tr,),
        in_specs=[pl.BlockSpec((tr, 1024), lambda i: (i, C_Q // 1024)),
                  pl.BlockSpec((tr, 1536), lambda i: (i, C_KV // 1536)),
                  pl.BlockSpec((tr, HEAD_DIM), lambda i: (i, 0)),
                  pl.BlockSpec((tr, HEAD_DIM), lambda i: (i, 0)),
                  pl.BlockSpec((8, HEAD_DIM), lambda i: (0, 0))],
        out_specs=[pl.BlockSpec((tr, 1024), lambda i: (i, 0)),
                   pl.BlockSpec((tr, 1024), lambda i: (i, 0)),
                   pl.BlockSpec((tr, 512), lambda i: (i, 0)),
                   pl.BlockSpec((4, tr, HEAD_DIM), lambda i: (0, i, 0)),
                   pl.BlockSpec((8, tr, HEAD_DIM), lambda i: (0, i, 0))],
        out_shape=[jax.ShapeDtypeStruct((m, 1024), BF16),
                   jax.ShapeDtypeStruct((m, 1024), F32),
                   jax.ShapeDtypeStruct((m, 512), F32),
                   jax.ShapeDtypeStruct((4, m, HEAD_DIM), F32),
                   jax.ShapeDtypeStruct((8, m, HEAD_DIM), BF16)],
        compiler_params=_cparams(("parallel",)),
        name="nsa_prep",
    )(z_main, z_main, cos2, sin2, gain8)


def _compress_rows(load_pair, w_ref, c, n):
    acc = jnp.zeros((n, 256), F32)
    for j in range(8):
        xa, xb = load_pair(j)
        lhs = jnp.concatenate([xa, xb], axis=1).astype(BF16)
        acc = acc + jnp.dot(lhs, w_ref[c, j], preferred_element_type=F32)
    return acc


def _compress_const(pe_ref, w_ref, c):
    acc = jnp.zeros((16, 256), F32)
    for j in range(8):
        acc = acc + jnp.dot(pe_ref[c, j].astype(BF16), w_ref[c, j], preferred_element_type=F32)
    return acc[0:1, 0:128] + acc[1:2, 128:256]


def _compress_prompt_kernel(x_ref, w_ref, pe_ref, r_ref, c_ref, *, n):
    c = pl.program_id(1) // 2

    def load_pair(j):
        return (x_ref[0, 0, pl.ds(2 * j, n, stride=16), :],
                x_ref[0, 0, pl.ds(2 * j + 1, n, stride=16), :])

    r_ref[0, 0, 0:n, :] = _compress_rows(load_pair, w_ref, c, n)
    r_ref[0, 0, n:n + 8, :] = jnp.zeros((8, 256), F32)
    c_ref[0, 0] = jnp.broadcast_to(_compress_const(pe_ref, w_ref, c), (8, 128))


def _compress_prompt(cmpin, wcat, pe_lhs, bsz, t):
    n = t // CMP_STRIDE
    x = cmpin.reshape(4, bsz, t, HEAD_DIM)
    return pl.pallas_call(
        functools.partial(_compress_prompt_kernel, n=n),
        grid=(bsz, 4),
        in_specs=[pl.BlockSpec((1, 1, t, HEAD_DIM), lambda b, cg: (cg, b, 0, 0)),
                  pl.BlockSpec((2, 8, 256, 256), lambda b, cg: (0, 0, 0, 0)),
                  pl.BlockSpec((2, 8, 16, 256), lambda b, cg: (0, 0, 0, 0))],
        out_specs=[pl.BlockSpec((1, 1, n + 8, 256), lambda b, cg: (b, cg, 0, 0)),
                   pl.BlockSpec((1, 1, 8, 128), lambda b, cg: (b, cg, 0, 0))],
        out_shape=[jax.ShapeDtypeStruct((bsz, 4, n + 8, 256), F32),
                   jax.ShapeDtypeStruct((bsz, 4, 8, 128), F32)],
        compiler_params=_cparams(("parallel", "parallel")),
        name="compress_prompt",
    )(x, wcat, pe_lhs)


def _compress_paged_kernel(pt_ref, cache_ref, w_ref, pe_ref, r_ref, c_ref, buf, sem, *, layer, pp, ns):
    b = pl.program_id(0)
    s = pl.program_id(1)
    step = b * ns + s
    nsteps = pl.num_programs(0) * ns
    slot = step % 2
    rows = pp * 128
    n = pp * 8

    def copies(bb, ss, sl):
        out = []
        for j in range(pp):
            page = pt_ref[bb, ss * pp + j]
            for cg in range(4):
                out.append(pltpu.make_async_copy(
                    cache_ref.at[layer, page, :, pl.ds(cg * HEAD_DIM, HEAD_DIM)],
                    buf.at[sl, cg, pl.ds(j * 128, 128), :],
                    sem.at[sl]))
        return out

    @pl.when(step == 0)
    def _():
        for cp in copies(b, s, slot):
            cp.start()

    @pl.when(step + 1 < nsteps)
    def _():
        nxt = step + 1
        for cp in copies(nxt // ns, nxt % ns, 1 - slot):
            cp.start()

    for cp in copies(b, s, slot):
        cp.wait()

    @pl.when(s == 0)
    def _():
        r_ref[0, :, ns * n:ns * n + 8, :] = jnp.zeros((4, 8, 256), F32)
        for cg in range(4):
            c_ref[0, cg] = jnp.broadcast_to(_compress_const(pe_ref, w_ref, cg // 2), (8, 128))

    for cg in range(4):
        def load_pair(j, cg=cg):
            return (buf[slot, cg, pl.ds(2 * j, n, stride=16), :],
                    buf[slot, cg, pl.ds(2 * j + 1, n, stride=16), :])
        r_ref[0, cg, pl.ds(pl.multiple_of(s * n, n), n), :] = _compress_rows(load_pair, w_ref, cg // 2, n)


def _compress_paged(page_table, cache4, wcat, pe_lhs, layer, pp):
    dec_b, n_pages = page_table.shape
    ns = n_pages // pp
    n_tot = n_pages * 8
    grid_spec = pltpu.PrefetchScalarGridSpec(
        num_scalar_prefetch=1,
        grid=(dec_b, ns),
        in_specs=[pl.BlockSpec(memory_space=pl.ANY),
                  pl.BlockSpec((2, 8, 256, 256), lambda b, s, pt: (0, 0, 0, 0)),
                  pl.BlockSpec((2, 8, 16, 256), lambda b, s, pt: (0, 0, 0, 0))],
        out_specs=[pl.BlockSpec((1, 4, n_tot + 8, 256), lambda b, s, pt: (b, 0, 0, 0)),
                   pl.BlockSpec((1, 4, 8, 128), lambda b, s, pt: (b, 0, 0, 0))],
        scratch_shapes=[pltpu.VMEM((2, 4, pp * 128, HEAD_DIM), F32), pltpu.SemaphoreType.DMA((2,))],
    )
    return pl.pallas_call(
        functools.partial(_compress_paged_kernel, layer=layer, pp=pp, ns=ns),
        grid_spec=grid_spec,
        out_shape=[jax.ShapeDtypeStruct((dec_b, 4, n_tot + 8, 256), F32),
                   jax.ShapeDtypeStruct((dec_b, 4, 8, 128), F32)],
        compiler_params=_cparams(("arbitrary", "arbitrary")),
        name="compress_paged",
    )(page_table, cache4, wcat, pe_lhs)


def _combine_cmp(r, cc, n, n_cmp):
    a = r[0:n, 0:128]
    bsh = r[1:n + 1, 128:256]
    ci = lax.broadcasted_iota(jnp.int32, (n, 1), 0)
    return jnp.where(ci < n_cmp, a + bsh + cc, 0.0).astype(BF16)


def _softmax_rows(s, mask):
    s = jnp.where(mask, s, -jnp.inf)
    m = jnp.max(s, axis=-1, keepdims=True)
    m = jnp.where(m > -jnp.inf, m, 0.0)
    e = jnp.exp(s - m)
    return e / jnp.maximum(jnp.sum(e, axis=-1, keepdims=True), 1e-30)


def _cmp_and_select(q4, kc, vc, ov, t_q, n, n_cmp, n_slc, n_sel, nsp):
    nq = t_q.shape[0]
    t_rows = jnp.concatenate([t_q] * GQA_REP, axis=0)
    sc = _nt(q4, kc) * SCALE
    ci = lax.broadcasted_iota(jnp.int32, (1, n), 1)
    cmask = (ci * CMP_STRIDE + (CMP_LEN - 1) <= t_rows) & (ci < n_cmp)
    pc = _softmax_rows(sc, cmask)
    o_cmp = jnp.dot(pc.astype(BF16), vc, preferred_element_type=F32)
    pcs = pc[0:nq] + pc[nq:2 * nq] + pc[2 * nq:3 * nq] + pc[3 * nq:4 * nq]
    hi = pcs.astype(BF16)
    lo = (pcs - hi.astype(F32)).astype(BF16)
    imp = jnp.dot(hi, ov, preferred_element_type=F32) + jnp.dot(lo, ov, preferred_element_type=F32)
    j = lax.broadcasted_iota(jnp.int32, (1, nsp), 1)
    cur = t_q // SLC_BLK
    forced = (j == 0) | (j == cur) | (j == cur - 1)
    valid = (j * SLC_BLK <= t_q) & (j < n_slc)
    score = jnp.where(valid, imp + SEL_BONUS * forced.astype(F32), -jnp.inf)
    rank = jnp.zeros((nq, nsp), F32)
    for k in range(n_slc):
        col = score[:, k:k + 1]
        before = (col > score) | ((col == score) & (k < j))
        rank = rank + before.astype(F32)
    sel = (rank < float(n_sel)) & (j < n_slc)
    return o_cmp, sel.astype(BF16)


def _flash_update(q4, k, v, mask4, m, l, acc):
    s = jnp.where(mask4, _nt(q4, k) * SCALE, -jnp.inf)
    m_new = jnp.maximum(m, jnp.max(s, axis=-1, keepdims=True))
    p = jnp.exp(s - m_new)
    alpha = jnp.exp(m - m_new)
    l_new = alpha * l + jnp.sum(p, axis=-1, keepdims=True)
    acc_new = alpha * acc + jnp.dot(p.astype(BF16), v, preferred_element_type=F32)
    return m_new, l_new, acc_new


def _sel_mask(sel, blk0, t_q, kpos0, nk, nsp):
    jb = lax.broadcasted_iota(jnp.int32, (nsp, nk), 0)
    kk = lax.broadcasted_iota(jnp.int32, (nsp, nk), 1)
    expand = (jb == blk0 + kk // SLC_BLK).astype(BF16)
    hit = jnp.dot(sel, expand, preferred_element_type=F32)
    kpos = kpos0 + lax.broadcasted_iota(jnp.int32, (1, nk), 1)
    return (hit > 0.5) & (kpos <= t_q)


def _gate_and_write(o_ref_write, gts, nz, o_cmp, o_slc, o_win, nq):
    for r in range(GQA_REP):
        rs = slice(r * nq, (r + 1) * nq)
        o = (gts[:, 3 * r:3 * r + 1] * o_cmp[rs] + gts[:, 3 * r + 1:3 * r + 2] * o_slc[rs]
             + gts[:, 3 * r + 2:3 * r + 3] * o_win[rs])
        o_ref_write(r, o * _silu(nz[:, r * HEAD_DIM:(r + 1) * HEAD_DIM]))


def _nsa_prompt_kernel(q_ref, rk_ref, rv_ref, ck_ref, cv_ref, ks_ref, vs_ref, kw_ref, vw_ref, ov_ref,
                       gz_ref, nz_ref, o_ref, *, n, n_cmp, n_slc, n_sel, nsp):
    i = pl.program_id(2)
    nq = Q_BLOCK
    qb = q_ref[0]
    q4 = jnp.concatenate([qb[:, r * HEAD_DIM:(r + 1) * HEAD_DIM] for r in range(GQA_REP)], axis=0)
    t_q = i * nq + lax.broadcasted_iota(jnp.int32, (nq, 1), 0)

    kc = _combine_cmp(rk_ref.at[0, 0], ck_ref[0, 0, 0:1, :], n, n_cmp)
    vc = _combine_cmp(rv_ref.at[0, 0], cv_ref[0, 0, 0:1, :], n, n_cmp)
    o_cmp, sel = _cmp_and_select(q4, kc, vc, ov_ref[...], t_q, n, n_cmp, n_slc, n_sel, nsp)

    init = (jnp.full((GQA_REP * nq, 1), NEG_BIG, F32), jnp.zeros((GQA_REP * nq, 1), F32),
            jnp.zeros((GQA_REP * nq, HEAD_DIM), F32))

    def slc_body(kb, carry):
        off = pl.multiple_of(kb * nq, nq)
        k = ks_ref[0, 0, pl.ds(off, nq), :]
        v = vs_ref[0, 0, pl.ds(off, nq), :]
        mq = _sel_mask(sel, 2 * kb, t_q, kb * nq, nq, nsp)
        return _flash_update(q4, k, v, jnp.concatenate([mq] * GQA_REP, axis=0), *carry)

    m, l, acc = lax.fori_loop(0, i + 1, slc_body, init)
    o_slc = acc / jnp.maximum(l, 1e-30)

    def win_body(kb, carry):
        off = pl.multiple_of(kb * nq, nq)
        k = kw_ref[0, 0, pl.ds(off, nq), :]
        v = vw_ref[0, 0, pl.ds(off, nq), :]
        d = t_q - (kb * nq + lax.broadcasted_iota(jnp.int32, (1, nq), 1))
        mq = (d >= 0) & (d < WINDOW)
        return _flash_update(q4, k, v, jnp.concatenate([mq] * GQA_REP, axis=0), *carry)

    m, l, acc = lax.fori_loop(jnp.maximum(i - WINDOW // nq, 0), i + 1, win_body, init)
    o_win = acc / jnp.maximum(l, 1e-30)

    gts = jax.nn.sigmoid(gz_ref[0])
    nz = nz_ref[0]

    def write(r, val):
        o_ref[0, :, r * HEAD_DIM:(r + 1) * HEAD_DIM] = val.astype(o_ref.dtype)

    _gate_and_write(write, gts, nz, o_cmp, o_slc, o_win, nq)


def _nsa_prompt(q_r, r_cmp, c_cmp, kv_att, ov, z_small, z_main, bsz, t):
    n = t // CMP_STRIDE
    n_cmp = n - CMP_LEN // CMP_STRIDE + 1
    n_slc = -(-t // SLC_BLK)
    nsp = ov.shape[1]
    q3 = q_r.reshape(bsz, t, 1024)
    kv4 = kv_att.reshape(8, bsz, t, HEAD_DIM)
    zs3 = z_small.reshape(bsz, t, N_SMALL)
    zm3 = z_main.reshape(bsz, t, N_MAIN)
    rspec = lambda off: pl.BlockSpec((1, 1, n + 8, 256), lambda b, g, i: (b, off + g, 0, 0))
    cspec = lambda off: pl.BlockSpec((1, 1, 8, 128), lambda b, g, i: (b, off + g, 0, 0))
    kvspec = lambda off: pl.BlockSpec((1, 1, t, HEAD_DIM), lambda b, g, i: (off + g, b, 0, 0))
    out = pl.pallas_call(
        functools.partial(_nsa_prompt_kernel, n=n, n_cmp=n_cmp, n_slc=n_slc, n_sel=min(N_SEL, n_slc), nsp=nsp),
        grid=(bsz, N_KV_HEADS, t // Q_BLOCK),
        in_specs=[pl.BlockSpec((1, Q_BLOCK, 512), lambda b, g, i: (b, i, g)),
                  rspec(0), rspec(2), cspec(0), cspec(2),
                  kvspec(0), kvspec(2), kvspec(4), kvspec(6),
                  pl.BlockSpec(ov.shape, lambda b, g, i: (0, 0)),
                  pl.BlockSpec((1, Q_BLOCK, 128), lambda b, g, i: (b, i, 1 + g)),
                  pl.BlockSpec((1, Q_BLOCK, 512), lambda b, g, i: (b, i, C_NSAZ // 512 + g))],
        out_specs=pl.BlockSpec((1, Q_BLOCK, 512), lambda b, g, i: (b, i, g)),
        out_shape=jax.ShapeDtypeStruct((bsz, t, 1024), BF16),
        compiler_params=_cparams(("parallel", "parallel", "arbitrary")),
        name="nsa_prompt",
    )(q3, r_cmp, r_cmp, c_cmp, c_cmp, kv4, kv4, kv4, kv4, ov, zs3, zm3)
    return out.reshape(bsz * t, 1024)


def _nsa_decode_kernel(pt_ref, q_ref, r_ref, cc_ref, page_ref, new_ref, cw_ref, ov_ref, zs_ref, nz_ref, o_ref,
                       q_scr, ocmp_scr, sel_scr, m_scr, l_scr, acc_scr,
                       *, n, n_cmp, n_slc, n_sel, nsp, past_len, n_pages, nq):
    p = pl.program_id(1)
    t_q = past_len + lax.broadcasted_iota(jnp.int32, (nq, 1), 0)

    @pl.when(p == 0)
    def _():
        qb = q_ref[0]
        for g in range(N_KV_HEADS):
            q4 = jnp.concatenate([qb[:, (g * GQA_REP + r) * HEAD_DIM:(g * GQA_REP + r + 1) * HEAD_DIM]
                                  for r in range(GQA_REP)], axis=0)
            q_scr[g] = q4
            kc = _combine_cmp(r_ref.at[0, g], cc_ref[0, g, 0:1, :], n, n_cmp)
            vc = _combine_cmp(r_ref.at[0, 2 + g], cc_ref[0, 2 + g, 0:1, :], n, n_cmp)
            o_cmp, sel = _cmp_and_select(q4, kc, vc, ov_ref[...], t_q, n, n_cmp, n_slc, n_sel, nsp)
            ocmp_scr[g] = o_cmp
            sel_scr[g] = sel.astype(F32)
            m_scr[g] = jnp.full((GQA_REP * nq, 1), NEG_BIG, F32)
            l_scr[g] = jnp.zeros((GQA_REP * nq, 1), F32)
            acc_scr[g] = jnp.zeros((GQA_REP * nq, HEAD_DIM), F32)

    def update(g, k, v, blk0, kpos0):
        mq = _sel_mask(sel_scr[g].astype(BF16), blk0, t_q, kpos0, 128, nsp)
        m, l, acc = _flash_update(q_scr[g], k, v, jnp.concatenate([mq] * GQA_REP, axis=0),
                                  m_scr[g], l_scr[g], acc_scr[g])
        m_scr[g] = m
        l_scr[g] = l
        acc_scr[g] = acc

    @pl.when(p < n_pages)
    def _():
        for g in range(N_KV_HEADS):
            k = page_ref[0, 0, :, g * HEAD_DIM:(g + 1) * HEAD_DIM].astype(BF16)
            v = page_ref[0, 0, :, 256 + g * HEAD_DIM:256 + (g + 1) * HEAD_DIM].astype(BF16)
            update(g, k, v, 2 * p, p * 128)

    @pl.when(p == n_pages)
    def _():
        gts_all = jax.nn.sigmoid(zs_ref[0])
        nz_all = nz_ref[0]
        for g in range(N_KV_HEADS):
            update(g, new_ref[g, 0], new_ref[2 + g, 0], past_len // SLC_BLK, past_len)
            o_slc = acc_scr[g] / jnp.maximum(l_scr[g], 1e-30)
            q4 = q_scr[g]
            t_rows = jnp.concatenate([t_q] * GQA_REP, axis=0)
            kw = cw_ref[0, 0, :, g * HEAD_DIM:(g + 1) * HEAD_DIM].astype(BF16)
            vw = cw_ref[0, 0, :, 256 + g * HEAD_DIM:256 + (g + 1) * HEAD_DIM].astype(BF16)
            d1 = t_rows - (past_len - WINDOW + lax.broadcasted_iota(jnp.int32, (1, WINDOW), 1))
            s1 = jnp.where((d1 >= 0) & (d1 < WINDOW), _nt(q4, kw) * SCALE, -jnp.inf)
            d2 = t_rows - (past_len + lax.broadcasted_iota(jnp.int32, (1, 128), 1))
            s2 = jnp.where((d2 >= 0) & (d2 < WINDOW), _nt(q4, new_ref[4 + g, 0]) * SCALE, -jnp.inf)
            mx = jnp.maximum(jnp.max(s1, axis=-1, keepdims=True), jnp.max(s2, axis=-1, keepdims=True))
            mx = jnp.where(mx > -jnp.inf, mx, 0.0)
            p1 = jnp.exp(s1 - mx)
            p2 = jnp.exp(s2 - mx)
            den = jnp.sum(p1, axis=-1, keepdims=True) + jnp.sum(p2, axis=-1, keepdims=True)
            o_win = (jnp.dot(p1.astype(BF16), vw, preferred_element_type=F32)
                     + jnp.dot(p2.astype(BF16), new_ref[6 + g, 0], preferred_element_type=F32))
            o_win = o_win / jnp.maximum(den, 1e-30)

            def write(r, val, g=g):
                c0 = (g * GQA_REP + r) * HEAD_DIM
                o_ref[0, :, c0:c0 + HEAD_DIM] = val.astype(o_ref.dtype)

            _gate_and_write(write, gts_all[:, 128 * (1 + g):128 * (2 + g)],
                            nz_all[:, g * 512:(g + 1) * 512], ocmp_scr[g], o_slc, o_win, nq)


def _nsa_decode(page_table, q_s, r_cmp, c_cmp, cache4, new_att, cache_win4, ov, z_small, z_main, layer,
                past_len, nq):
    dec_b, n_pages = page_table.shape
    n = n_pages * 8
    tk = past_len + 4
    n_cmp = tk // CMP_STRIDE - CMP_LEN // CMP_STRIDE + 1
    n_slc = -(-tk // SLC_BLK)
    nsp = ov.shape[1]
    q3 = q_s.reshape(dec_b, nq, 1024)
    zs3 = z_small.reshape(dec_b, nq, N_SMALL)
    zm3 = z_main.reshape(dec_b, nq, N_MAIN)
    grid_spec = pltpu.PrefetchScalarGridSpec(
        num_scalar_prefetch=1,
        grid=(dec_b, n_pages + 1),
        in_specs=[pl.BlockSpec((1, nq, 1024), lambda b, p, pt: (b, 0, 0)),
                  pl.BlockSpec((1, 4, n + 8, 256), lambda b, p, pt: (b, 0, 0, 0)),
                  pl.BlockSpec((1, 4, 8, 128), lambda b, p, pt: (b, 0, 0, 0)),
                  pl.BlockSpec((1, 1, 128, 512),
                               lambda b, p, pt: (layer, pt[b, jnp.minimum(p, n_pages - 1)], 0, 1)),
                  pl.BlockSpec((8, 1, 128, HEAD_DIM), lambda b, p, pt: (0, b, 0, 0)),
                  pl.BlockSpec((1, 1, WINDOW, 512), lambda b, p, pt: (layer, b, 0, 0)),
                  pl.BlockSpec(ov.shape, lambda b, p, pt: (0, 0)),
                  pl.BlockSpec((1, nq, N_SMALL), lambda b, p, pt: (b, 0, 0)),
                  pl.BlockSpec((1, nq, 1024), lambda b, p, pt: (b, 0, C_NSAZ // 1024))],
        out_specs=pl.BlockSpec((1, nq, 1024), lambda b, p, pt: (b, 0, 0)),
        scratch_shapes=[pltpu.VMEM((2, GQA_REP * nq, HEAD_DIM), BF16),
                        pltpu.VMEM((2, GQA_REP * nq, HEAD_DIM), F32),
                        pltpu.VMEM((2, nq, nsp), F32),
                        pltpu.VMEM((2, GQA_REP * nq, 1), F32),
                        pltpu.VMEM((2, GQA_REP * nq, 1), F32),
                        pltpu.VMEM((2, GQA_REP * nq, HEAD_DIM), F32)],
    )
    out = pl.pallas_call(
        functools.partial(_nsa_decode_kernel, n=n, n_cmp=n_cmp, n_slc=n_slc, n_sel=min(N_SEL, n_slc), nsp=nsp,
                          past_len=past_len, n_pages=n_pages, nq=nq),
        grid_spec=grid_spec,
        out_shape=jax.ShapeDtypeStruct((dec_b, nq, 1024), BF16),
        compiler_params=_cparams(("arbitrary", "arbitrary")),
        name="nsa_decode",
    )(page_table, q3, r_cmp, c_cmp, cache4, new_att, cache_win4, ov, zs3, zm3)
    return out.reshape(dec_b * nq, 1024)


def _pool_kernel(u_ref, pz_ref, pre_ref, w_ref, sc_ref, o_ref, buf, *, tr, pos0):
    i = pl.program_id(1)

    @pl.when(i == 0)
    def _():
        buf[0:16, :] = pre_ref[0]

    u = u_ref[0]
    buf[16:16 + tr, :] = u
    pos = pos0 + i * tr + lax.broadcasted_iota(jnp.int32, (tr, 1), 0)
    for gi, w in enumerate(POOL_WINDOWS):
        cs = slice(gi * POOL_GROUP, (gi + 1) * POOL_GROUP)
        s = buf[16:16 + tr, cs]
        for k in range(1, w):
            s = s + buf[16 - k:16 - k + tr, cs]
        cnt = jnp.minimum(w, pos + 1).astype(F32)
        pooled = s / cnt - u[:, cs]
        mixed = jnp.dot(pooled.astype(BF16), w_ref[gi], preferred_element_type=F32) * sc_ref[:, cs]
        o_ref[0, :, cs] = (mixed * _silu(pz_ref[0, :, cs])).astype(o_ref.dtype)
    buf[0:16, :] = buf[tr:tr + 16, :]


def _pool(z_main, prefix16, pool_w, pool_scale, bsz, t, tr, pos0):
    zm3 = z_main.reshape(bsz, t, N_MAIN)
    out = pl.pallas_call(
        functools.partial(_pool_kernel, tr=tr, pos0=pos0),
        grid=(bsz, t // tr),
        in_specs=[pl.BlockSpec((1, tr, 1024), lambda b, i: (b, i, C_POOLU // 1024)),
                  pl.BlockSpec((1, tr, 1024), lambda b, i: (b, i, C_POOLZ // 1024)),
                  pl.BlockSpec((1, 16, 1024), lambda b, i: (b, 0, 0)),
                  pl.BlockSpec((4, POOL_GROUP, POOL_GROUP), lambda b, i: (0, 0, 0)),
                  pl.BlockSpec((1, 1024), lambda b, i: (0, 0))],
        out_specs=pl.BlockSpec((1, tr, 1024), lambda b, i: (b, i, 0)),
        out_shape=jax.ShapeDtypeStruct((bsz, t, 1024), BF16),
        scratch_shapes=[pltpu.VMEM((16 + tr, 1024), F32)],
        compiler_params=_cparams(("arbitrary", "arbitrary")),
        name="pool",
    )(zm3, zm3, prefix16, pool_w, pool_scale.reshape(1, 1024))
    return out.reshape(bsz * t, 1024)


def _ssd_kernel(xs_ref, bm_ref, cm_ref, dt_ref, mz_ref, pre_ref, init_ref, cw_ref, cb_ref, dtb_ref, alx_ref,
                alp_ref, dsk_ref, mn_ref, ex_ref, y_ref, fin_ref, xbuf, st, *, rows_in, valid_len):
    c = pl.program_id(1)
    nc = pl.num_programs(1)
    L = SSD_CHUNK

    @pl.when(c == 0)
    def _():
        xbuf[0:8, :] = pre_ref[0]
        for pr in range(8):
            st[:, pr * 128:(pr + 1) * 128] = init_ref[0, pr * 128:(pr + 1) * 128, :].T

    if rows_in < L:
        xbuf[8 + rows_in:8 + L, :] = jnp.zeros((L - rows_in, CONV_CH), F32)
    xbuf[8:8 + rows_in, 0:1024] = xs_ref[0]
    xbuf[8:8 + rows_in, 1024:1280] = bm_ref[0]
    xbuf[8:8 + rows_in, 1280:1536] = cm_ref[0]
    acc = cb_ref[...] + xbuf[5:5 + L, :] * cw_ref[0:1, :]
    for j in range(1, CONV_W):
        acc = acc + xbuf[5 + j:5 + j + L, :] * cw_ref[j:j + 1, :]
    xbuf[0:8, :] = xbuf[L:L + 8, :]
    xc = _silu(acc)
    xs = xc[:, 0:1024]
    bmat = xc[:, 1024:1280]
    cmat = xc[:, 1280:1536]

    if rows_in < L:
        dtz = jnp.concatenate([dt_ref[0], jnp.zeros((L - rows_in, 128), F32)], axis=0)
    else:
        dtz = dt_ref[0]
    dtz = dtz + dtb_ref[...]
    dt = jnp.maximum(dtz, 0.0) + jnp.log1p(jnp.exp(-jnp.abs(dtz)))
    row = c * L + lax.broadcasted_iota(jnp.int32, (L, 1), 0)
    dt = jnp.where(row < valid_len, dt, 0.0)

    li = lax.broadcasted_iota(jnp.int32, (L, L), 0)
    si = lax.broadcasted_iota(jnp.int32, (L, L), 1)
    tril_b = li >= si
    tril = tril_b.astype(F32)
    dt_x = jnp.dot(dt, ex_ref[...], precision=HI, preferred_element_type=F32)
    la_x = dt_x * (-jnp.exp(alx_ref[...]))
    acs_x = jnp.dot(tril, la_x, precision=HI, preferred_element_type=F32)
    la_p = dt * (-jnp.exp(alp_ref[...]))
    acs_pt = jnp.dot(tril, la_p, precision=HI, preferred_element_type=F32).T

    xdt = xs * dt_x
    a_end = acs_x[L - 1:L, :]
    eacs = jnp.exp(acs_x)
    xw = (xdt * jnp.exp(a_end - acs_x)).astype(BF16)
    xdt_b = xdt.astype(BF16)
    lane = lax.broadcasted_iota(jnp.int32, (1, 128), 1)
    y_parts = []
    for g in range(M_GROUPS):
        bg = bmat[:, g * M_STATE:(g + 1) * M_STATE]
        cg = cmat[:, g * M_STATE:(g + 1) * M_STATE].astype(BF16)
        gmat = _nt(cg, bg.astype(BF16))
        bgt = bg.T.astype(BF16)
        for hp in range(M_HEADS // M_GROUPS // 2):
            pr = g * (M_HEADS // M_GROUPS // 2) + hp
            cs = slice(pr * 128, (pr + 1) * 128)
            xpair = xdt_b[:, cs]
            ydiag = jnp.zeros((L, 128), F32)
            for half in range(2):
                h = 2 * pr + half
                col = acs_x[:, h * M_HEAD_DIM:h * M_HEAD_DIM + 1]
                rw = acs_pt[h:h + 1, :]
                dec = jnp.where(tril_b, jnp.exp(col - rw), 0.0)
                mm = (gmat * dec).astype(BF16)
                in_half = (lane >= half * M_HEAD_DIM) & (lane < (half + 1) * M_HEAD_DIM)
                rhs = jnp.where(in_half, xpair, jnp.zeros_like(xpair))
                ydiag = ydiag + jnp.dot(mm, rhs, preferred_element_type=F32)
            prev = st[:, cs]
            yoff = jnp.dot(cg, prev.astype(BF16), preferred_element_type=F32) * eacs[:, cs]
            st[:, cs] = jnp.exp(a_end[:, cs]) * prev + jnp.dot(bgt, xw[:, cs], preferred_element_type=F32)
            y_parts.append(ydiag + yoff)
    y = jnp.concatenate(y_parts, axis=1)
    y = y + dsk_ref[...] * xs
    if rows_in < L:
        mz = jnp.concatenate([mz_ref[0], jnp.zeros((L - rows_in, 1024), F32)], axis=0)
    else:
        mz = mz_ref[0]
    y = y * _silu(mz)
    half_w = BRANCH_WIDTH // M_GROUPS
    outs = []
    for g in range(M_GROUPS):
        yg = y[:, g * half_w:(g + 1) * half_w]
        outs.append(yg * lax.rsqrt(jnp.mean(yg * yg, axis=-1, keepdims=True) + EPS))
    yn = jnp.concatenate(outs, axis=1) * mn_ref[...]
    y_ref[0] = yn[0:rows_in].astype(y_ref.dtype)

    @pl.when(c == nc - 1)
    def _():
        for pr in range(8):
            fin_ref[0, pr * 128:(pr + 1) * 128, :] = st[:, pr * 128:(pr + 1) * 128].T


def _ssd(z_main, z_small, prefix8, init, consts, bsz, t, rows_in, valid_len):
    conv_w8, conv_b, dtb, alx, alp, dskx, mnorm, expand = consts
    zm3 = z_main.reshape(bsz, t, N_MAIN)
    zs3 = z_small.reshape(bsz, t, N_SMALL)
    nc = t // rows_in
    full = lambda shp: pl.BlockSpec(shp, lambda b, c: tuple(0 for _ in shp))
    y, fin = pl.pallas_call(
        functools.partial(_ssd_kernel, rows_in=rows_in, valid_len=valid_len),
        grid=(bsz, nc),
        in_specs=[pl.BlockSpec((1, rows_in, 1024), lambda b, c: (b, c, C_XS // 1024)),
                  pl.BlockSpec((1, rows_in, 256), lambda b, c: (b, c, C_B // 256)),
                  pl.BlockSpec((1, rows_in, 256), lambda b, c: (b, c, C_C // 256)),
                  pl.BlockSpec((1, rows_in, 128), lambda b, c: (b, c, 0)),
                  pl.BlockSpec((1, rows_in, 1024), lambda b, c: (b, c, C_MZ // 1024)),
                  pl.BlockSpec((1, 8, CONV_CH), lambda b, c: (b, 0, 0)),
                  pl.BlockSpec((1, 1024, M_STATE), lambda b, c: (b, 0, 0)),
                  full((8, CONV_CH)), full((1, CONV_CH)), full((1, 128)), full((1, 1024)), full((1, 128)),
                  full((1, 1024)), full((1, 1024)), full((128, 1024))],
        out_specs=[pl.BlockSpec((1, rows_in, 1024), lambda b, c: (b, c, 0)),
                   pl.BlockSpec((1, 1024, M_STATE), lambda b, c: (b, 0, 0))],
        out_shape=[jax.ShapeDtypeStruct((bsz, t, 1024), BF16),
                   jax.ShapeDtypeStruct((bsz, 1024, M_STATE), F32)],
        scratch_shapes=[pltpu.VMEM((8 + SSD_CHUNK, CONV_CH), F32), pltpu.VMEM((M_STATE, 1024), F32)],
        compiler_params=_cparams(("arbitrary", "arbitrary")),
        name="ssd",
    )(zm3, zm3, zm3, zs3, zm3, prefix8, init, conv_w8, conv_b, dtb, alx, alp, dskx, mnorm, expand)
    return y.reshape(bsz * t, 1024), fin


def _overlap_matrix(n, n_cmp, n_slc, nsp):
    i = np.arange(n)[:, None]
    j = np.arange(nsp)[None, :]
    ov = (i * CMP_STRIDE < j * SLC_BLK + SLC_BLK) & (i * CMP_STRIDE + CMP_LEN > j * SLC_BLK)
    ov = ov & (i < n_cmp) & (j < n_slc)
    return jnp.asarray(ov.astype(np.float32), dtype=BF16)


def _rope_tables(pos):
    inv = ROPE_THETA ** (-jnp.arange(0, HEAD_DIM, 2, dtype=F32) / HEAD_DIM)
    ang = pos.astype(F32)[:, None] * inv[None, :]
    cos, sin = jnp.cos(ang), jnp.sin(ang)
    return jnp.concatenate([cos, cos], axis=1), jnp.concatenate([-sin, sin], axis=1)


def _layer_consts(l, w_in, qk_gain, cmp_pe, cmp_w, pool_w, conv_w, conv_b, dt_bias, a_log, d_skip, mnorm_w,
                  w_branch, w_out):
    offs = np.cumsum((0, 1024, 1536, 24, 1024, 1024, 1024, 1024, CONV_CH, M_HEADS, 3 * D_MODEL))
    seg = lambda k: w_in[l][:, offs[k]:offs[k + 1]]
    xbc = seg(7)
    w_main = jnp.concatenate([seg(0), seg(3), seg(4), seg(5), seg(6), xbc[:, :1024], seg(9), seg(1),
                              xbc[:, 1024:1280], xbc[:, 1280:1536]], axis=1).astype(BF16)
    gates = seg(2)
    zpad = lambda w: jnp.zeros((D_MODEL, w), F32)
    w_small = jnp.concatenate([seg(8), zpad(128 - M_HEADS), gates[:, :12], zpad(116), gates[:, 12:], zpad(116)],
                              axis=1).astype(BF16)
    gain8 = jnp.concatenate([qk_gain[l], jnp.zeros((4, HEAD_DIM), F32)], axis=0)
    w4 = cmp_w[l].reshape(2, 2, 8, 2, HEAD_DIM, HEAD_DIM)
    wcat = jnp.transpose(w4, (0, 2, 3, 4, 1, 5)).reshape(2, 8, 256, 256).astype(BF16)
    pe4 = cmp_pe[l].reshape(2, 2, 8, 2 * HEAD_DIM)
    pe_lhs = jnp.concatenate([jnp.transpose(pe4, (0, 2, 1, 3)), jnp.zeros((2, 8, 14, 256), F32)], axis=2)
    rep = lambda v: jnp.repeat(v, M_HEAD_DIM).reshape(1, 1024)
    pad128 = lambda v: jnp.concatenate([v, jnp.zeros((128 - v.shape[0],), F32)]).reshape(1, 128)
    expand = (np.arange(128)[:, None] == (np.arange(1024)[None, :] // M_HEAD_DIM)).astype(np.float32)
    ssd_consts = (jnp.concatenate([conv_w[l], jnp.zeros((4, CONV_CH), F32)], axis=0), conv_b[l].reshape(1, CONV_CH),
                  pad128(dt_bias[l]), rep(a_log[l]), pad128(a_log[l]), rep(d_skip[l]), mnorm_w[l].reshape(1, 1024),
                  jnp.asarray(expand))
    return dict(w_main=w_main, w_small=w_small, gain8=gain8, wcat=wcat, pe_lhs=pe_lhs,
                pool_w=pool_w[l].astype(BF16), ssd=ssd_consts, w_branch=w_branch[l].astype(BF16),
                w_out=w_out[l].astype(BF16))


def _xbc_cols(z3):
    return jnp.concatenate([z3[..., C_XS:C_XS + 1024], z3[..., C_B:C_B + 256], z3[..., C_C:C_C + 256]], axis=-1)


def _tile(m, pref):
    for t in pref:
        if m % t == 0:
            return t
    return m


def kernel(x_prompt, x_sample, cache_kv, cache_win, state_pool, state_conv, state_ssm, page_table, norm_w, w_in,
           qk_gain, cmp_pe, cmp_w, pool_w, pool_scale, conv_w, conv_b, dt_bias, a_log, d_skip, mnorm_w, w_branch,
           w_out):
    depth = w_in.shape[0]
    bsz, seq, _ = x_prompt.shape
    dec_b, dec_t, _ = x_sample.shape
    n_pool, page = cache_kv.shape[1], cache_kv.shape[2]
    n_pages = page_table.shape[1]
    past_len = n_pages * page
    assert page == 128 and dec_t == 4 and cache_win.shape[2] == WINDOW and seq % 128 == 0
    nq = 8
    cache4 = cache_kv.reshape(depth, n_pool, page, 4 * N_KV_HEADS * HEAD_DIM)
    cache_win4 = cache_win.reshape(depth, dec_b, WINDOW, 2 * N_KV_HEADS * HEAD_DIM)

    mp = bsz * seq
    ms = dec_b * nq
    xp = x_prompt.reshape(mp, D_MODEL)
    xs = jnp.pad(x_sample, ((0, 0), (0, nq - dec_t), (0, 0))).reshape(ms, D_MODEL)

    cos_p, sin_p = _rope_tables(jnp.tile(jnp.arange(seq), bsz))
    cos_s, sin_s = _rope_tables(jnp.tile(past_len + jnp.arange(nq), dec_b))

    n_p = seq // CMP_STRIDE
    ov_p = _overlap_matrix(n_p, n_p - 1, -(-seq // SLC_BLK), 128)
    n_s = n_pages * 8
    tk = past_len + dec_t
    n_slc_s = -(-tk // SLC_BLK)
    nsp_s = -(-n_slc_s // 128) * 128
    ov_s = _overlap_matrix(n_s, tk // CMP_STRIDE - 1, n_slc_s, nsp_s)

    tm_p = _tile(mp, (512, 256, 128))
    outs_p = [[] for _ in range(5)]
    outs_s = [[] for _ in range(5)]
    for l in range(depth):
        c = _layer_consts(l, w_in, qk_gain, cmp_pe, cmp_w, pool_w, conv_w, conv_b, dt_bias, a_log, d_skip,
                          mnorm_w, w_branch, w_out)
        h = _rmsnorm(xp, norm_w[l], tm_p)
        z_main = _matmul(h, c["w_main"], tm_p, 1024, "inproj")
        z_small = _matmul(h, c["w_small"], tm_p, N_SMALL, "inproj_small")
        q_r, rows, win, cmpin, kv_att = _prep(z_main, cos_p, sin_p, c["gain8"], _tile(mp, (256, 128)))
        r_cmp, c_cmp = _compress_prompt(cmpin, c["wcat"], c["pe_lhs"], bsz, seq)
        nsa_out = _nsa_prompt(q_r, r_cmp, c_cmp, kv_att, ov_p, z_small, z_main, bsz, seq)
        pool_out = _pool(z_main, jnp.zeros((bsz, 16, 1024), F32), c["pool_w"], pool_scale[l], bsz, seq,
                         _tile(seq, (256, 128)), 0)
        m_out, fin = _ssd(z_main, z_small, jnp.zeros((bsz, 8, CONV_CH), F32),
                          jnp.zeros((bsz, 1024, M_STATE), F32), c["ssd"], bsz, seq, SSD_CHUNK, seq)
        merged = _merge(nsa_out, pool_out, m_out, c["w_branch"], z_main, tm_p, 512)
        xp_new = _outproj(xp, merged, c["w_out"], tm_p, 512)
        zm3 = z_main.reshape(bsz, seq, N_MAIN)
        outs_p[0].append(rows.reshape(bsz, seq, 4, N_KV_HEADS, HEAD_DIM))
        outs_p[1].append(win.reshape(bsz, seq, 2, N_KV_HEADS, HEAD_DIM)[:, seq - min(WINDOW, seq):])
        outs_p[2].append(zm3[:, seq - POOL_STATE:, C_POOLU:C_POOLU + 1024])
        outs_p[3].append(_xbc_cols(zm3[:, seq - (CONV_W - 1):]))
        outs_p[4].append(fin.reshape(bsz, M_HEADS, M_HEAD_DIM, M_STATE))
        xp = xp_new
        h = _rmsnorm(xs, norm_w[l], ms)
        z_main = _matmul(h, c["w_main"], ms, 1024, "inproj_s")
        z_small = _matmul(h, c["w_small"], ms, N_SMALL, "inproj_small_s")
        q_r, rows, win, _, kv_att = _prep(z_main, cos_s, sin_s, c["gain8"], ms)
        r_cmp, c_cmp = _compress_paged(page_table, cache4, c["wcat"], c["pe_lhs"], l,
                                       _tile(n_pages, (32, 16, 8, 4, 2)))
        new_att = jnp.pad(kv_att.reshape(8, dec_b, nq, HEAD_DIM), ((0, 0), (0, 0), (0, 128 - nq), (0, 0)))
        new_att = jnp.where((jnp.arange(128) < dec_t)[None, None, :, None], new_att, jnp.zeros_like(new_att))
        nsa_out = _nsa_decode(page_table, q_r, r_cmp, c_cmp, cache4, new_att, cache_win4, ov_s, z_small, z_main,
                              l, past_len, nq)
        pre16 = jnp.pad(state_pool[l], ((0, 0), (16 - POOL_STATE, 0), (0, 0)))
        pool_out = _pool(z_main, pre16, c["pool_w"], pool_scale[l], dec_b, nq, nq, past_len)
        pre8 = jnp.pad(state_conv[l], ((0, 0), (8 - (CONV_W - 1), 0), (0, 0)))
        m_out, fin = _ssd(z_main, z_small, pre8, state_ssm[l].reshape(dec_b, 1024, M_STATE), c["ssd"], dec_b, nq,
                          nq, dec_t)
        merged = _merge(nsa_out, pool_out, m_out, c["w_branch"], z_main, ms, 512)
        xs_new = _outproj(xs, merged, c["w_out"], ms, 512)
        zm3 = z_main.reshape(dec_b, nq, N_MAIN)
        outs_s[0].append(rows.reshape(dec_b, nq, 4, N_KV_HEADS, HEAD_DIM)[:, :dec_t])
        win_new = win.reshape(dec_b, nq, 2, N_KV_HEADS, HEAD_DIM)[:, :dec_t]
        outs_s[1].append(jnp.concatenate([cache_win[l], win_new], axis=1)[:, -WINDOW:])
        outs_s[2].append(jnp.concatenate([state_pool[l], zm3[:, :dec_t, C_POOLU:C_POOLU + 1024]],
                                         axis=1)[:, -POOL_STATE:])
        outs_s[3].append(jnp.concatenate([state_conv[l], _xbc_cols(zm3[:, :dec_t])], axis=1)[:, -(CONV_W - 1):])
        outs_s[4].append(fin.reshape(dec_b, M_HEADS, M_HEAD_DIM, M_STATE))
        xs = xs_new

    y_p = xp.reshape(bsz, seq, D_MODEL)
    y_s = xs.reshape(dec_b, nq, D_MODEL)[:, :dec_t]
    stk = lambda lst: jnp.stack(lst)
    return (y_p, y_s, *[stk(v) for v in outs_p], *[stk(v) for v in outs_s])
```

```python
import functools

import numpy as np
import jax
import jax.numpy as jnp
from jax import lax
from jax.experimental import pallas as pl
from jax.experimental.pallas import tpu as pltpu

F32 = jnp.float32
BF16 = jnp.bfloat16

D_MODEL = 2048
BRANCH_WIDTH = D_MODEL // 2
HEAD_DIM = 128
N_HEADS = BRANCH_WIDTH // HEAD_DIM
GQA_REP = 4
N_KV_HEADS = N_HEADS // GQA_REP
ROPE_THETA = 10000.0
CMP_LEN = 32
CMP_STRIDE = 16
SLC_BLK = 64
SLC_SHIFT = SLC_BLK.bit_length() - 1
N_SEL = 16
WINDOW = 512
Q_BLOCK = 128
SEL_BONUS = 1.0e4
POOL_WINDOWS = (2, 4, 8, 16)
POOL_GROUP = BRANCH_WIDTH // len(POOL_WINDOWS)
POOL_STATE = max(POOL_WINDOWS) - 1
M_HEAD_DIM = 64
M_HEADS = BRANCH_WIDTH // M_HEAD_DIM
M_STATE = 128
M_GROUPS = 2
CONV_W = 4
CONV_CH = BRANCH_WIDTH + 2 * M_GROUPS * M_STATE
SSD_CHUNK = 128
EPS = 1e-6
SCALE = HEAD_DIM ** -0.5
NEG_BIG = -1.0e30

C_Q = 0
C_NSAZ = 1024
C_POOLU = 2048
C_POOLZ = 3072
C_MZ = 4096
C_XS = 5120
C_MG = 6144
C_KV = 12288
C_B = 13824
C_C = 14080
N_MAIN = 14336
N_SMALL = 384

VMEM_LIMIT = 56 * 1024 * 1024
HI = lax.Precision.HIGHEST


def _cparams(sem):
    return pltpu.CompilerParams(dimension_semantics=sem, vmem_limit_bytes=VMEM_LIMIT)


def _nt(a, b):
    return lax.dot_general(a, b, (((1,), (1,)), ((), ())), preferred_element_type=F32)


def _silu(x):
    return x * jax.nn.sigmoid(x)


def _rmsnorm_kernel(x_ref, g_ref, o_ref):
    x = x_ref[...]
    ms = jnp.mean(x * x, axis=-1, keepdims=True)
    o_ref[...] = (x * lax.rsqrt(ms + EPS) * g_ref[...]).astype(o_ref.dtype)


def _rmsnorm(x, g, tr):
    m, d = x.shape
    return pl.pallas_call(
        _rmsnorm_kernel,
        grid=(m // tr,),
        in_specs=[pl.BlockSpec((tr, d), lambda i: (i, 0)), pl.BlockSpec((1, d), lambda i: (0, 0))],
        out_specs=pl.BlockSpec((tr, d), lambda i: (i, 0)),
        out_shape=jax.ShapeDtypeStruct((m, d), BF16),
        compiler_params=_cparams(("parallel",)),
        name="rmsnorm",
    )(x, g.reshape(1, d))


def _mm_kernel(a_ref, b_ref, o_ref):
    o_ref[...] = jnp.dot(a_ref[...], b_ref[...], preferred_element_type=F32).astype(o_ref.dtype)


def _matmul(a, b, tm, tn, name):
    m, k = a.shape
    n = b.shape[1]
    return pl.pallas_call(
        _mm_kernel,
        grid=(m // tm, n // tn),
        in_specs=[pl.BlockSpec((tm, k), lambda i, j: (i, 0)), pl.BlockSpec((k, tn), lambda i, j: (0, j))],
        out_specs=pl.BlockSpec((tm, tn), lambda i, j: (i, j)),
        out_shape=jax.ShapeDtypeStruct((m, n), F32),
        compiler_params=_cparams(("parallel", "parallel")),
        name=name,
    )(a, b)


def _merge_kernel(a_ref, p_ref, m_ref, w_ref, g0_ref, g1_ref, g2_ref, o_ref):
    acc = jax.nn.sigmoid(g0_ref[...]) * jnp.dot(a_ref[...], w_ref[0], preferred_element_type=F32)
    acc = acc + jax.nn.sigmoid(g1_ref[...]) * jnp.dot(p_ref[...], w_ref[1], preferred_element_type=F32)
    acc = acc + jax.nn.sigmoid(g2_ref[...]) * jnp.dot(m_ref[...], w_ref[2], preferred_element_type=F32)
    o_ref[...] = acc.astype(o_ref.dtype)


def _merge(nsa_out, pool_out, m_out, w_branch, z_main, tm, tn):
    m = nsa_out.shape[0]
    gblk = [(C_MG + b * D_MODEL) // tn for b in range(3)]
    act = pl.BlockSpec((tm, BRANCH_WIDTH), lambda i, j: (i, 0))
    return pl.pallas_call(
        _merge_kernel,
        grid=(m // tm, D_MODEL // tn),
        in_specs=[act, act, act,
                  pl.BlockSpec((3, BRANCH_WIDTH, tn), lambda i, j: (0, 0, j)),
                  pl.BlockSpec((tm, tn), lambda i, j: (i, gblk[0] + j)),
                  pl.BlockSpec((tm, tn), lambda i, j: (i, gblk[1] + j)),
                  pl.BlockSpec((tm, tn), lambda i, j: (i, gblk[2] + j))],
        out_specs=pl.BlockSpec((tm, tn), lambda i, j: (i, j)),
        out_shape=jax.ShapeDtypeStruct((m, D_MODEL), BF16),
        compiler_params=_cparams(("parallel", "parallel")),
        name="merge",
    )(nsa_out, pool_out, m_out, w_branch, z_main, z_main, z_main)


def _outproj_kernel(x_ref, a_ref, w_ref, o_ref):
    o_ref[...] = x_ref[...] + jnp.dot(a_ref[...], w_ref[...], preferred_element_type=F32)


def _outproj(x, merged, w_out, tm, tn):
    m = x.shape[0]
    return pl.pallas_call(
        _outproj_kernel,
        grid=(m // tm, D_MODEL // tn),
        in_specs=[pl.BlockSpec((tm, tn), lambda i, j: (i, j)),
                  pl.BlockSpec((tm, D_MODEL), lambda i, j: (i, 0)),
                  pl.BlockSpec((D_MODEL, tn), lambda i, j: (0, j))],
        out_specs=pl.BlockSpec((tm, tn), lambda i, j: (i, j)),
        out_shape=jax.ShapeDtypeStruct((m, D_MODEL), F32),
        compiler_params=_cparams(("parallel", "parallel")),
        name="outproj",
    )(x, merged, w_out)


def _prep_kernel(zq_ref, zkv_ref, cos_ref, sin_ref, gain_ref, q_ref, rows_ref, win_ref, cmpin_ref, kvatt_ref,
                 *maybe_vt_ref):
    cos = cos_ref[...]
    sin = sin_ref[...]

    def norm_rope(x, gi):
        ms = jnp.mean(x * x, axis=-1, keepdims=True)
        y = x * lax.rsqrt(ms + EPS) * gain_ref[gi:gi + 1, :]
        return y * cos + pltpu.roll(y, HEAD_DIM // 2, 1) * sin

    for h in range(N_HEADS):
        sl = slice(h * HEAD_DIM, (h + 1) * HEAD_DIM)
        q_ref[:, sl] = norm_rope(zq_ref[:, sl], 0).astype(q_ref.dtype)
    for br in range(3):
        for g in range(N_KV_HEADS):
            ck = br * 512 + g * HEAD_DIM
            cv = br * 512 + 256 + g * HEAD_DIM
            k = norm_rope(zkv_ref[:, ck:ck + HEAD_DIM], 1 + br)
            v = zkv_ref[:, cv:cv + HEAD_DIM]
            if br < 2:
                rows_ref[:, ck:ck + HEAD_DIM] = k
                rows_ref[:, cv:cv + HEAD_DIM] = v
            else:
                win_ref[:, g * HEAD_DIM:(g + 1) * HEAD_DIM] = k
                win_ref[:, 256 + g * HEAD_DIM:256 + (g + 1) * HEAD_DIM] = v
            if br == 0:
                cmpin_ref[g] = k
                cmpin_ref[2 + g] = v
            else:
                base = (br - 1) * 4
                kvatt_ref[base + g] = k.astype(kvatt_ref.dtype)
                kvatt_ref[base + 2 + g] = v.astype(kvatt_ref.dtype)
                if maybe_vt_ref:
                    maybe_vt_ref[0][(br - 1) * 2 + g, 0] = v.T.astype(kvatt_ref.dtype)


def _prep(z_main, cos2, sin2, gain8, tr, emit_vt):
    m = z_main.shape[0]
    vt_specs = [pl.BlockSpec((4, 1, HEAD_DIM, tr), lambda i: (0, i, 0, 0))] if emit_vt else []
    vt_shapes = [jax.ShapeDtypeStruct((4, m // tr, HEAD_DIM, tr), BF16)] if emit_vt else []
    return pl.pallas_call(
        _prep_kernel,
        grid=(m // tr,),
        in_specs=[pl.BlockSpec((tr, 1024), lambda i: (i, C_Q // 1024)),
                  pl.BlockSpec((tr, 1536), lambda i: (i, C_KV // 1536)),
                  pl.BlockSpec((tr, HEAD_DIM), lambda i: (i, 0)),
                  pl.BlockSpec((tr, HEAD_DIM), lambda i: (i, 0)),
                  pl.BlockSpec((8, HEAD_DIM), lambda i: (0, 0))],
        out_specs=[pl.BlockSpec((tr, 1024), lambda i: (i, 0)),
                   pl.BlockSpec((tr, 1024), lambda i: (i, 0)),
                   pl.BlockSpec((tr, 512), lambda i: (i, 0)),
                   pl.BlockSpec((4, tr, HEAD_DIM), lambda i: (0, i, 0)),
                   pl.BlockSpec((8, tr, HEAD_DIM), lambda i: (0, i, 0))] + vt_specs,
        out_shape=[jax.ShapeDtypeStruct((m, 1024), BF16),
                   jax.ShapeDtypeStruct((m, 1024), F32),
                   jax.ShapeDtypeStruct((m, 512), F32),
                   jax.ShapeDtypeStruct((4, m, HEAD_DIM), F32),
                   jax.ShapeDtypeStruct((8, m, HEAD_DIM), BF16)] + vt_shapes,
        compiler_params=_cparams(("parallel",)),
        name="nsa_prep",
    )(z_main, z_main, cos2, sin2, gain8)


def _compress_rows(load_pair, w_ref, c, n):
    acc = jnp.zeros((n, 256), F32)
    for j in range(8):
        xa, xb = load_pair(j)
        lhs = jnp.concatenate([xa, xb], axis=1).astype(BF16)
        acc = acc + jnp.dot(lhs, w_ref[c, j], preferred_element_type=F32)
    return acc


def _compress_const(pe_ref, w_ref, c):
    acc = jnp.zeros((16, 256), F32)
    for j in range(8):
        acc = acc + jnp.dot(pe_ref[c, j].astype(BF16), w_ref[c, j], preferred_element_type=F32)
    return acc[0:1, 0:128] + acc[1:2, 128:256]


def _compress_prompt_kernel(x_ref, w_ref, pe_ref, r_ref, c_ref, *, n):
    c = pl.program_id(1) // 2

    def load_pair(j):
        return (x_ref[0, 0, pl.ds(2 * j, n, stride=16), :],
                x_ref[0, 0, pl.ds(2 * j + 1, n, stride=16), :])

    r_ref[0, 0, 0:n, :] = _compress_rows(load_pair, w_ref, c, n)
    r_ref[0, 0, n:n + 8, :] = jnp.zeros((8, 256), F32)
    c_ref[0, 0] = jnp.broadcast_to(_compress_const(pe_ref, w_ref, c), (8, 128))


def _compress_prompt(cmpin, wcat, pe_lhs, bsz, t):
    n = t // CMP_STRIDE
    x = cmpin.reshape(4, bsz, t, HEAD_DIM)
    return pl.pallas_call(
        functools.partial(_compress_prompt_kernel, n=n),
        grid=(bsz, 4),
        in_specs=[pl.BlockSpec((1, 1, t, HEAD_DIM), lambda b, cg: (cg, b, 0, 0)),
                  pl.BlockSpec((2, 8, 256, 256), lambda b, cg: (0, 0, 0, 0)),
                  pl.BlockSpec((2, 8, 16, 256), lambda b, cg: (0, 0, 0, 0))],
        out_specs=[pl.BlockSpec((1, 1, n + 8, 256), lambda b, cg: (b, cg, 0, 0)),
                   pl.BlockSpec((1, 1, 8, 128), lambda b, cg: (b, cg, 0, 0))],
        out_shape=[jax.ShapeDtypeStruct((bsz, 4, n + 8, 256), F32),
                   jax.ShapeDtypeStruct((bsz, 4, 8, 128), F32)],
        compiler_params=_cparams(("parallel", "parallel")),
        name="compress_prompt",
    )(x, wcat, pe_lhs)


def _combine_cmp(r, cc, n, n_cmp):
    a = r[0:n, 0:128]
    bsh = r[1:n + 1, 128:256]
    ci = lax.broadcasted_iota(jnp.int32, (n, 1), 0)
    return jnp.where(ci < n_cmp, a + bsh + cc, 0.0)


def _softmax_rows(s, mask):
    s = jnp.where(mask, s, -jnp.inf)
    m = jnp.max(s, axis=-1, keepdims=True)
    m = jnp.where(m > -jnp.inf, m, 0.0)
    e = jnp.exp(s - m)
    return e / jnp.maximum(jnp.sum(e, axis=-1, keepdims=True), 1e-30)


def _cmp_and_select(q4, kc, vc, ov, t_q, n, n_cmp, n_slc, n_sel, nsp):
    nq = t_q.shape[0]
    t_rows = jnp.concatenate([t_q] * GQA_REP, axis=0)
    sc = _nt(q4, kc) * SCALE
    ci = lax.broadcasted_iota(jnp.int32, (1, n), 1)
    cmask = (ci * CMP_STRIDE + (CMP_LEN - 1) <= t_rows) & (ci < n_cmp)
    pc = _softmax_rows(sc, cmask)
    o_cmp = jnp.dot(pc.astype(BF16), vc, preferred_element_type=F32)
    pcs = pc[0:nq] + pc[nq:2 * nq] + pc[2 * nq:3 * nq] + pc[3 * nq:4 * nq]
    hi = pcs.astype(BF16)
    lo = (pcs - hi.astype(F32)).astype(BF16)
    imp = jnp.dot(hi, ov, preferred_element_type=F32) + jnp.dot(lo, ov, preferred_element_type=F32)
    j = lax.broadcasted_iota(jnp.int32, (1, nsp), 1)
    cur = t_q >> SLC_SHIFT
    forced = (j == 0) | (j == cur) | (j == cur - 1)
    valid = (j * SLC_BLK <= t_q) & (j < n_slc)
    score = jnp.where(valid, imp + SEL_BONUS * forced.astype(F32), -jnp.inf)
    rank = jnp.zeros((nq, nsp), F32)
    for k in range(n_slc):
        col = score[:, k:k + 1]
        before = (col > score) | ((col == score) & (k < j))
        rank = rank + before.astype(F32)
    sel = (rank < float(n_sel)) & (j < n_slc)
    return o_cmp, sel.astype(BF16)


def _sel_mask(sel_win, win0, blk0, t_q, kpos0, nk):
    jb = win0 + lax.broadcasted_iota(jnp.int32, (128, nk), 0)
    kk = lax.broadcasted_iota(jnp.int32, (128, nk), 1)
    expand = (jb == blk0 + (kk >> SLC_SHIFT)).astype(BF16)
    hit = jnp.dot(sel_win, expand, preferred_element_type=F32)
    kpos = kpos0 + lax.broadcasted_iota(jnp.int32, (1, nk), 1)
    return (hit > 0.5) & (kpos <= t_q)


def _gate_and_write(o_ref_write, gts, nz, o_cmp, o_slc, o_win, nq):
    for r in range(GQA_REP):
        rs = slice(r * nq, (r + 1) * nq)
        o = (gts[:, 3 * r:3 * r + 1] * o_cmp[rs] + gts[:, 3 * r + 1:3 * r + 2] * o_slc[rs]
             + gts[:, 3 * r + 2:3 * r + 3] * o_win[rs])
        o_ref_write(r, o * _silu(nz[:, r * HEAD_DIM:(r + 1) * HEAD_DIM]))


def _nsa_prompt_kernel(q_ref, rk_ref, rv_ref, ck_ref, cv_ref, ks_ref, vst_ref, kw_ref, vwt_ref, ovt_ref,
                       gz_ref, nz_ref, o_ref, *, n, n_cmp, n_slc, n_sel, nsb, tk):
    i = pl.program_id(2)
    nq = Q_BLOCK
    nr = GQA_REP * nq
    qb = q_ref[0]
    q4 = jnp.concatenate([qb[:, r * HEAD_DIM:(r + 1) * HEAD_DIM] for r in range(GQA_REP)], axis=0)
    t_q = i * nq + lax.broadcasted_iota(jnp.int32, (1, nq), 1)
    t_q4 = jnp.concatenate([t_q] * GQA_REP, axis=1)

    kc = _combine_cmp(rk_ref.at[0, 0], ck_ref[0, 0, 0:1, :], n, n_cmp).astype(BF16)
    vct = _combine_cmp(rv_ref.at[0, 0], cv_ref[0, 0, 0:1, :], n, n_cmp).T.astype(BF16)
    sc = _nt(kc, q4) * SCALE
    ci = lax.broadcasted_iota(jnp.int32, (n, 1), 0)
    cmask = (ci * CMP_STRIDE + (CMP_LEN - 1) <= t_q4) & (ci < n_cmp)
    sc = jnp.where(cmask, sc, -jnp.inf)
    mx = jnp.max(sc, axis=0, keepdims=True)
    mx = jnp.where(mx > -jnp.inf, mx, 0.0)
    e = jnp.exp(sc - mx)
    pc = e * (1.0 / jnp.maximum(jnp.sum(e, axis=0, keepdims=True), 1e-30))
    o_cmp = jnp.dot(vct, pc.astype(BF16), preferred_element_type=F32)
    pcs = pc[:, 0:nq] + pc[:, nq:2 * nq] + pc[:, 2 * nq:3 * nq] + pc[:, 3 * nq:4 * nq]
    hi = pcs.astype(BF16)
    lo = (pcs - hi.astype(F32)).astype(BF16)
    ovt = ovt_ref[...]
    imp = jnp.dot(ovt, hi, preferred_element_type=F32) + jnp.dot(ovt, lo, preferred_element_type=F32)

    jb = lax.broadcasted_iota(jnp.int32, (nsb, 1), 0)
    cur = t_q >> SLC_SHIFT
    forced = (jb == 0) | (jb == cur) | (jb == cur - 1)
    valid = (jb * SLC_BLK <= t_q) & (jb < n_slc)
    score = jnp.where(valid, imp + SEL_BONUS * forced.astype(F32), -jnp.inf)
    rank = jnp.zeros((nsb, nq), F32)
    for k in range(n_slc):
        row = score[k:k + 1, :]
        before = (row > score) | ((row == score) & (k < jb))
        rank = rank + before.astype(F32)
    sel = ((rank < float(n_sel)) & (jb < n_slc)).astype(BF16)
    if nsb < 128:
        sel = jnp.concatenate([sel, jnp.zeros((128 - nsb, nq), BF16)], axis=0)

    def sweep(k_ref, vt_ref, lo_tile, hi_tile, mask_fn):
        def body(kb, carry):
            m, l, acc = carry
            k = k_ref[0, 0, pl.ds(pl.multiple_of(kb * tk, tk), tk), :]
            vt = vt_ref[0, 0, kb]
            kpos = kb * tk + lax.broadcasted_iota(jnp.int32, (tk, 1), 0)
            mq = mask_fn(kb, kpos)
            st = _nt(k, q4) * SCALE
            st = jnp.concatenate([jnp.where(mq, st[:, r * nq:(r + 1) * nq], -jnp.inf)
                                  for r in range(GQA_REP)], axis=1)
            m_new = jnp.maximum(m, jnp.max(st, axis=0, keepdims=True))
            p = jnp.exp(st - m_new)
            alpha = jnp.exp(m - m_new)
            l_new = alpha * l + jnp.sum(p, axis=0, keepdims=True)
            acc_new = alpha * acc + jnp.dot(vt, p.astype(BF16), preferred_element_type=F32)
            return m_new, l_new, acc_new

        init = (jnp.full((1, nr), NEG_BIG, F32), jnp.zeros((1, nr), F32), jnp.zeros((HEAD_DIM, nr), F32))
        m, l, acc = lax.fori_loop(lo_tile, hi_tile, body, init)
        return acc * (1.0 / jnp.maximum(l, 1e-30))

    def slc_mask(kb, kpos):
        jl = lax.broadcasted_iota(jnp.int32, (tk, 128), 1)
        expand = (jl == (kpos >> SLC_SHIFT)).astype(BF16)
        hit = jnp.dot(expand, sel, preferred_element_type=F32)
        return (hit > 0.5) & (kpos <= t_q)

    def win_mask(kb, kpos):
        d = t_q - kpos
        return (d >= 0) & (d < WINDOW)

    last_tile = (i * nq + nq - 1) // tk + 1
    o_slc = sweep(ks_ref, vst_ref, 0, last_tile, slc_mask)
    o_win = sweep(kw_ref, vwt_ref, jnp.maximum(i * nq - (WINDOW - 1), 0) // tk, last_tile, win_mask)

    gt = jax.nn.sigmoid(gz_ref[0]).T
    nz = nz_ref[0]
    for r in range(GQA_REP):
        cs = slice(r * nq, (r + 1) * nq)
        ot = (gt[3 * r:3 * r + 1, :] * o_cmp[:, cs] + gt[3 * r + 1:3 * r + 2, :] * o_slc[:, cs]
              + gt[3 * r + 2:3 * r + 3, :] * o_win[:, cs])
        hs = slice(r * HEAD_DIM, (r + 1) * HEAD_DIM)
        o_ref[0, :, hs] = (ot.T * _silu(nz[:, hs])).astype(o_ref.dtype)


def _nsa_prompt(q_r, r_cmp, c_cmp, kv_att, v_t, ovt, z_small, z_main, bsz, t, tk):
    n = t // CMP_STRIDE
    n_cmp = n - CMP_LEN // CMP_STRIDE + 1
    n_slc = -(-t // SLC_BLK)
    nsb = ovt.shape[0]
    q3 = q_r.reshape(bsz, t, 1024)
    kv4 = kv_att.reshape(8, bsz, t, HEAD_DIM)
    vt5 = v_t.reshape(4, bsz, t // tk, HEAD_DIM, tk)
    zs3 = z_small.reshape(bsz, t, N_SMALL)
    zm3 = z_main.reshape(bsz, t, N_MAIN)
    rspec = lambda off: pl.BlockSpec((1, 1, n + 8, 256), lambda b, g, i: (b, off + g, 0, 0))
    cspec = lambda off: pl.BlockSpec((1, 1, 8, 128), lambda b, g, i: (b, off + g, 0, 0))
    kspec = lambda off: pl.BlockSpec((1, 1, t, HEAD_DIM), lambda b, g, i: (off + g, b, 0, 0))
    vtspec = lambda off: pl.BlockSpec((1, 1, t // tk, HEAD_DIM, tk), lambda b, g, i: (off + g, b, 0, 0, 0))
    out = pl.pallas_call(
        functools.partial(_nsa_prompt_kernel, n=n, n_cmp=n_cmp, n_slc=n_slc, n_sel=min(N_SEL, n_slc), nsb=nsb,
                          tk=tk),
        grid=(bsz, N_KV_HEADS, t // Q_BLOCK),
        in_specs=[pl.BlockSpec((1, Q_BLOCK, 512), lambda b, g, i: (b, i, g)),
                  rspec(0), rspec(2), cspec(0), cspec(2),
                  kspec(0), vtspec(0), kspec(4), vtspec(2),
                  pl.BlockSpec(ovt.shape, lambda b, g, i: (0, 0)),
                  pl.BlockSpec((1, Q_BLOCK, 128), lambda b, g, i: (b, i, 1 + g)),
                  pl.BlockSpec((1, Q_BLOCK, 512), lambda b, g, i: (b, i, C_NSAZ // 512 + g))],
        out_specs=pl.BlockSpec((1, Q_BLOCK, 512), lambda b, g, i: (b, i, g)),
        out_shape=jax.ShapeDtypeStruct((bsz, t, 1024), BF16),
        compiler_params=_cparams(("parallel", "parallel", "arbitrary")),
        name="nsa_prompt",
    )(q3, r_cmp, r_cmp, c_cmp, c_cmp, kv4, vt5, kv4, vt5, ovt, zs3, zm3)
    return out.reshape(bsz * t, 1024)


def _nsa_decode_kernel(pt_ref, cache_ref, q_ref, w_ref, pe_ref, new_ref, cw_ref, ov_ref, zs_ref, nz_ref, o_ref,
                       buf, sem, r_scr, sc_scr, v_scr,
                       *, layer, pp, ns, n, n_cmp, n_slc, n_sel, nsp, past_len, nq):
    b = pl.program_id(0)
    s = pl.program_id(1)
    step = b * ns + s
    nsteps = pl.num_programs(0) * ns
    slot = step % 2
    nch = pp * 8
    nk = pp * 128
    nr = GQA_REP * nq

    def copies(bb, ss, sl):
        return [pltpu.make_async_copy(cache_ref.at[layer, pt_ref[bb, ss * pp + j]],
                                      buf.at[sl, pl.ds(j * 1024, 1024), :], sem.at[sl]) for j in range(pp)]

    @pl.when(step == 0)
    def _():
        for cp in copies(b, s, slot):
            cp.start()

    @pl.when(step + 1 < nsteps)
    def _():
        nxt = step + 1
        for cp in copies(nxt // ns, nxt % ns, 1 - slot):
            cp.start()

    for cp in copies(b, s, slot):
        cp.wait()

    qb = q_ref[0]
    q4s = [jnp.concatenate([qb[:, (g * GQA_REP + r) * HEAD_DIM:(g * GQA_REP + r + 1) * HEAD_DIM]
                            for r in range(GQA_REP)], axis=0) for g in range(N_KV_HEADS)]

    @pl.when(s == 0)
    def _():
        r_scr[:, ns * nch:ns * nch + 8, :] = jnp.zeros((4, 8, 256), F32)

    row0 = pl.multiple_of(s * nch, nch)
    for c in range(2):
        acc = jnp.zeros((N_KV_HEADS * nch, 256), F32)
        for j in range(8):
            def chunk_rows(l, g):
                return buf[slot, pl.ds(l * 8 + c * 2 + g, nch, stride=128), :]
            lhs = jnp.concatenate(
                [jnp.concatenate([chunk_rows(2 * j, g), chunk_rows(2 * j + 1, g)], axis=1)
                 for g in range(N_KV_HEADS)], axis=0).astype(BF16)
            acc = acc + jnp.dot(lhs, w_ref[c, j], preferred_element_type=F32)
        for g in range(N_KV_HEADS):
            r_scr[c * 2 + g, pl.ds(row0, nch), :] = acc[g * nch:(g + 1) * nch]

    key0 = pl.multiple_of(s * nk, nk)
    for g in range(N_KV_HEADS):
        k = buf[slot, pl.ds(4 + g, nk, stride=8), :].astype(BF16)
        sc_scr[g, s] = _nt(q4s[g], k) * SCALE
        v_scr[g, pl.ds(key0, nk), :] = buf[slot, pl.ds(6 + g, nk, stride=8), :].astype(BF16)

    @pl.when(s == ns - 1)
    def _():
        t_q = past_len + lax.broadcasted_iota(jnp.int32, (nq, 1), 0)
        t_rows = jnp.concatenate([t_q] * GQA_REP, axis=0)
        gts_all = jax.nn.sigmoid(zs_ref[0])
        nz_all = nz_ref[0]
        cck = _compress_const(pe_ref, w_ref, 0)
        ccv = _compress_const(pe_ref, w_ref, 1)
        for g in range(N_KV_HEADS):
            q4 = q4s[g]
            kc = _combine_cmp(r_scr.at[g], cck, n, n_cmp).astype(BF16)
            vc = _combine_cmp(r_scr.at[2 + g], ccv, n, n_cmp).astype(BF16)
            o_cmp, sel = _cmp_and_select(q4, kc, vc, ov_ref[...], t_q, n, n_cmp, n_slc, n_sel, nsp)

            def chunk_mask(blk0, kpos0, nkeys):
                win0 = (blk0 // 128) * 128
                mq = _sel_mask(sel[:, win0:win0 + 128], win0, blk0, t_q, kpos0, nkeys)
                return jnp.concatenate([mq] * GQA_REP, axis=0)

            mx = jnp.full((nr, 1), -jnp.inf, F32)
            for cidx in range(ns):
                sm = jnp.where(chunk_mask(cidx * pp * 2, cidx * nk, nk), sc_scr[g, cidx], -jnp.inf)
                sc_scr[g, cidx] = sm
                mx = jnp.maximum(mx, jnp.max(sm, axis=-1, keepdims=True))
            s_new = jnp.where(chunk_mask(past_len // SLC_BLK, past_len, 128),
                              _nt(q4, new_ref[g, 0]) * SCALE, -jnp.inf)
            mx = jnp.maximum(mx, jnp.max(s_new, axis=-1, keepdims=True))
            mx = jnp.where(mx > -jnp.inf, mx, 0.0)
            p_new = jnp.exp(s_new - mx)
            den = jnp.sum(p_new, axis=-1, keepdims=True)
            acc = jnp.dot(p_new.astype(BF16), new_ref[2 + g, 0], preferred_element_type=F32)
            for cidx in range(ns):
                p = jnp.exp(sc_scr[g, cidx] - mx)
                den = den + jnp.sum(p, axis=-1, keepdims=True)
                acc = acc + jnp.dot(p.astype(BF16), v_scr[g, cidx * nk:(cidx + 1) * nk, :],
                                    preferred_element_type=F32)
            o_slc = acc / jnp.maximum(den, 1e-30)

            kw = cw_ref[0, 0, pl.ds(g, WINDOW, stride=4), :].astype(BF16)
            vw = cw_ref[0, 0, pl.ds(2 + g, WINDOW, stride=4), :].astype(BF16)
            d1 = t_rows - (past_len - WINDOW + lax.broadcasted_iota(jnp.int32, (1, WINDOW), 1))
            s1 = jnp.where((d1 >= 0) & (d1 < WINDOW), _nt(q4, kw) * SCALE, -jnp.inf)
            d2 = t_rows - (past_len + lax.broadcasted_iota(jnp.int32, (1, 128), 1))
            s2 = jnp.where((d2 >= 0) & (d2 < WINDOW), _nt(q4, new_ref[4 + g, 0]) * SCALE, -jnp.inf)
            mw = jnp.maximum(jnp.max(s1, axis=-1, keepdims=True), jnp.max(s2, axis=-1, keepdims=True))
            mw = jnp.where(mw > -jnp.inf, mw, 0.0)
            p1 = jnp.exp(s1 - mw)
            p2 = jnp.exp(s2 - mw)
            denw = jnp.sum(p1, axis=-1, keepdims=True) + jnp.sum(p2, axis=-1, keepdims=True)
            o_win = (jnp.dot(p1.astype(BF16), vw, preferred_element_type=F32)
                     + jnp.dot(p2.astype(BF16), new_ref[6 + g, 0], preferred_element_type=F32))
            o_win = o_win / jnp.maximum(denw, 1e-30)

            def write(r, val, g=g):
                c0 = (g * GQA_REP + r) * HEAD_DIM
                o_ref[0, :, c0:c0 + HEAD_DIM] = val.astype(o_ref.dtype)

            _gate_and_write(write, gts_all[:, 128 * (1 + g):128 * (2 + g)],
                            nz_all[:, g * 512:(g + 1) * 512], o_cmp, o_slc, o_win, nq)


def _nsa_decode(page_table, q_s, cache_pages, wcat, pe_lhs, new_att, cache_win_rows, ov, z_small, z_main, layer,
                past_len, nq, dec_t, pp):
    dec_b, n_pages = page_table.shape
    ns = n_pages // pp
    n = n_pages * 8
    tk = past_len + dec_t
    n_cmp = tk // CMP_STRIDE - CMP_LEN // CMP_STRIDE + 1
    n_slc = -(-tk // SLC_BLK)
    nsp = ov.shape[1]
    nr = GQA_REP * nq
    q3 = q_s.reshape(dec_b, nq, 1024)
    zs3 = z_small.reshape(dec_b, nq, N_SMALL)
    zm3 = z_main.reshape(dec_b, nq, N_MAIN)
    grid_spec = pltpu.PrefetchScalarGridSpec(
        num_scalar_prefetch=1,
        grid=(dec_b, ns),
        in_specs=[pl.BlockSpec(memory_space=pl.ANY),
                  pl.BlockSpec((1, nq, 1024), lambda b, s, pt: (b, 0, 0)),
                  pl.BlockSpec((2, 8, 256, 256), lambda b, s, pt: (0, 0, 0, 0)),
                  pl.BlockSpec((2, 8, 16, 256), lambda b, s, pt: (0, 0, 0, 0)),
                  pl.BlockSpec((8, 1, 128, HEAD_DIM), lambda b, s, pt: (0, b, 0, 0)),
                  pl.BlockSpec((1, 1, 4 * WINDOW, HEAD_DIM), lambda b, s, pt: (layer, b, 0, 0)),
                  pl.BlockSpec(ov.shape, lambda b, s, pt: (0, 0)),
                  pl.BlockSpec((1, nq, N_SMALL), lambda b, s, pt: (b, 0, 0)),
                  pl.BlockSpec((1, nq, 1024), lambda b, s, pt: (b, 0, C_NSAZ // 1024))],
        out_specs=pl.BlockSpec((1, nq, 1024), lambda b, s, pt: (b, 0, 0)),
        scratch_shapes=[pltpu.VMEM((2, pp * 1024, HEAD_DIM), F32),
                        pltpu.SemaphoreType.DMA((2,)),
                        pltpu.VMEM((4, n + 8, 256), F32),
                        pltpu.VMEM((N_KV_HEADS, ns, nr, pp * 128), F32),
                        pltpu.VMEM((N_KV_HEADS, n_pages * 128, HEAD_DIM), BF16)],
    )
    out = pl.pallas_call(
        functools.partial(_nsa_decode_kernel, layer=layer, pp=pp, ns=ns, n=n, n_cmp=n_cmp, n_slc=n_slc,
                          n_sel=min(N_SEL, n_slc), nsp=nsp, past_len=past_len, nq=nq),
        grid_spec=grid_spec,
        out_shape=jax.ShapeDtypeStruct((dec_b, nq, 1024), BF16),
        compiler_params=_cparams(("arbitrary", "arbitrary")),
        name="nsa_decode",
    )(page_table, cache_pages, q3, wcat, pe_lhs, new_att, cache_win_rows, ov, zs3, zm3)
    return out.reshape(dec_b * nq, 1024)


def _pool_kernel(u_ref, pz_ref, pre_ref, w_ref, sc_ref, o_ref, buf, *, tr, pos0):
    i = pl.program_id(1)

    @pl.when(i == 0)
    def _():
        buf[0:16, :] = pre_ref[0]

    u = u_ref[0]
    buf[16:16 + tr, :] = u
    pos = pos0 + i * tr + lax.broadcasted_iota(jnp.int32, (tr, 1), 0)
    for gi, w in enumerate(POOL_WINDOWS):
        cs = slice(gi * POOL_GROUP, (gi + 1) * POOL_GROUP)
        s = buf[16:16 + tr, cs]
        for k in range(1, w):
            s = s + buf[16 - k:16 - k + tr, cs]
        cnt = jnp.minimum(w, pos + 1).astype(F32)
        pooled = s / cnt - u[:, cs]
        mixed = jnp.dot(pooled.astype(BF16), w_ref[gi], preferred_element_type=F32) * sc_ref[:, cs]
        o_ref[0, :, cs] = (mixed * _silu(pz_ref[0, :, cs])).astype(o_ref.dtype)
    buf[0:16, :] = buf[tr:tr + 16, :]


def _pool(z_main, prefix16, pool_w, pool_scale, bsz, t, tr, pos0):
    zm3 = z_main.reshape(bsz, t, N_MAIN)
    out = pl.pallas_call(
        functools.partial(_pool_kernel, tr=tr, pos0=pos0),
        grid=(bsz, t // tr),
        in_specs=[pl.BlockSpec((1, tr, 1024), lambda b, i: (b, i, C_POOLU // 1024)),
                  pl.BlockSpec((1, tr, 1024), lambda b, i: (b, i, C_POOLZ // 1024)),
                  pl.BlockSpec((1, 16, 1024), lambda b, i: (b, 0, 0)),
                  pl.BlockSpec((4, POOL_GROUP, POOL_GROUP), lambda b, i: (0, 0, 0)),
                  pl.BlockSpec((1, 1024), lambda b, i: (0, 0))],
        out_specs=pl.BlockSpec((1, tr, 1024), lambda b, i: (b, i, 0)),
        out_shape=jax.ShapeDtypeStruct((bsz, t, 1024), BF16),
        scratch_shapes=[pltpu.VMEM((16 + tr, 1024), F32)],
        compiler_params=_cparams(("arbitrary", "arbitrary")),
        name="pool",
    )(zm3, zm3, prefix16, pool_w, pool_scale.reshape(1, 1024))
    return out.reshape(bsz * t, 1024)


def _ssd_kernel(xs_ref, bm_ref, cm_ref, dt_ref, mz_ref, pre_ref, init_ref, cw_ref, cb_ref, dtb_ref, alx_ref,
                alp_ref, dsk_ref, mn_ref, ex_ref, y_ref, fin_ref, xbuf, st, *, rows_in, valid_len):
    c = pl.program_id(1)
    nc = pl.num_programs(1)
    L = SSD_CHUNK

    @pl.when(c == 0)
    def _():
        xbuf[0:8, :] = pre_ref[0]
        for pr in range(8):
            st[:, pr * 128:(pr + 1) * 128] = init_ref[0, pr * 128:(pr + 1) * 128, :].T

    if rows_in < L:
        xbuf[8 + rows_in:8 + L, :] = jnp.zeros((L - rows_in, CONV_CH), F32)
    xbuf[8:8 + rows_in, 0:1024] = xs_ref[0]
    xbuf[8:8 + rows_in, 1024:1280] = bm_ref[0]
    xbuf[8:8 + rows_in, 1280:1536] = cm_ref[0]
    acc = cb_ref[...] + xbuf[5:5 + L, :] * cw_ref[0:1, :]
    for j in range(1, CONV_W):
        acc = acc + xbuf[5 + j:5 + j + L, :] * cw_ref[j:j + 1, :]
    xbuf[0:8, :] = xbuf[L:L + 8, :]
    xc = _silu(acc)
    xs = xc[:, 0:1024]
    bmat = xc[:, 1024:1280]
    cmat = xc[:, 1280:1536]

    if rows_in < L:
        dtz = jnp.concatenate([dt_ref[0], jnp.zeros((L - rows_in, 128), F32)], axis=0)
    else:
        dtz = dt_ref[0]
    dtz = dtz + dtb_ref[...]
    dt = jnp.maximum(dtz, 0.0) + jnp.log1p(jnp.exp(-jnp.abs(dtz)))
    row = c * L + lax.broadcasted_iota(jnp.int32, (L, 1), 0)
    dt = jnp.where(row < valid_len, dt, 0.0)

    li = lax.broadcasted_iota(jnp.int32, (L, L), 0)
    si = lax.broadcasted_iota(jnp.int32, (L, L), 1)
    tril_b = li >= si
    tril = tril_b.astype(F32)
    dt_x = jnp.dot(dt, ex_ref[...], precision=HI, preferred_element_type=F32)
    la_x = dt_x * (-jnp.exp(alx_ref[...]))
    acs_x = jnp.dot(tril, la_x, precision=HI, preferred_element_type=F32)
    la_p = dt * (-jnp.exp(alp_ref[...]))
    acs_pt = jnp.dot(tril, la_p, precision=HI, preferred_element_type=F32).T

    xdt = xs * dt_x
    a_end = acs_x[L - 1:L, :]
    eacs = jnp.exp(acs_x)
    xw = (xdt * jnp.exp(a_end - acs_x)).astype(BF16)
    xdt_b = xdt.astype(BF16)
    lane = lax.broadcasted_iota(jnp.int32, (1, 128), 1)
    y_parts = []
    for g in range(M_GROUPS):
        bg = bmat[:, g * M_STATE:(g + 1) * M_STATE]
        cg = cmat[:, g * M_STATE:(g + 1) * M_STATE].astype(BF16)
        gmat = _nt(cg, bg.astype(BF16))
        bgt = bg.T.astype(BF16)
        for hp in range(M_HEADS // M_GROUPS // 2):
            pr = g * (M_HEADS // M_GROUPS // 2) + hp
            cs = slice(pr * 128, (pr + 1) * 128)
            xpair = xdt_b[:, cs]
            ydiag = jnp.zeros((L, 128), F32)
            for half in range(2):
                h = 2 * pr + half
                col = acs_x[:, h * M_HEAD_DIM:h * M_HEAD_DIM + 1]
                rw = acs_pt[h:h + 1, :]
                dec = jnp.where(tril_b, jnp.exp(col - rw), 0.0)
                mm = (gmat * dec).astype(BF16)
                in_half = (lane >= half * M_HEAD_DIM) & (lane < (half + 1) * M_HEAD_DIM)
                rhs = jnp.where(in_half, xpair, jnp.zeros_like(xpair))
                ydiag = ydiag + jnp.dot(mm, rhs, preferred_element_type=F32)
            prev = st[:, cs]
            yoff = jnp.dot(cg, prev.astype(BF16), preferred_element_type=F32) * eacs[:, cs]
            st[:, cs] = jnp.exp(a_end[:, cs]) * prev + jnp.dot(bgt, xw[:, cs], preferred_element_type=F32)
            y_parts.append(ydiag + yoff)
    y = jnp.concatenate(y_parts, axis=1)
    y = y + dsk_ref[...] * xs
    if rows_in < L:
        mz = jnp.concatenate([mz_ref[0], jnp.zeros((L - rows_in, 1024), F32)], axis=0)
    else:
        mz = mz_ref[0]
    y = y * _silu(mz)
    half_w = BRANCH_WIDTH // M_GROUPS
    outs = []
    for g in range(M_GROUPS):
        yg = y[:, g * half_w:(g + 1) * half_w]
        outs.append(yg * lax.rsqrt(jnp.mean(yg * yg, axis=-1, keepdims=True) + EPS))
    yn = jnp.concatenate(outs, axis=1) * mn_ref[...]
    y_ref[0] = yn[0:rows_in].astype(y_ref.dtype)

    @pl.when(c == nc - 1)
    def _():
        for pr in range(8):
            fin_ref[0, pr * 128:(pr + 1) * 128, :] = st[:, pr * 128:(pr + 1) * 128].T


def _ssd(z_main, z_small, prefix8, init, consts, bsz, t, rows_in, valid_len):
    conv_w8, conv_b, dtb, alx, alp, dskx, mnorm, expand = consts
    zm3 = z_main.reshape(bsz, t, N_MAIN)
    zs3 = z_small.reshape(bsz, t, N_SMALL)
    nc = t // rows_in
    full = lambda shp: pl.BlockSpec(shp, lambda b, c: tuple(0 for _ in shp))
    y, fin = pl.pallas_call(
        functools.partial(_ssd_kernel, rows_in=rows_in, valid_len=valid_len),
        grid=(bsz, nc),
        in_specs=[pl.BlockSpec((1, rows_in, 1024), lambda b, c: (b, c, C_XS // 1024)),
                  pl.BlockSpec((1, rows_in, 256), lambda b, c: (b, c, C_B // 256)),
                  pl.BlockSpec((1, rows_in, 256), lambda b, c: (b, c, C_C // 256)),
                  pl.BlockSpec((1, rows_in, 128), lambda b, c: (b, c, 0)),
                  pl.BlockSpec((1, rows_in, 1024), lambda b, c: (b, c, C_MZ // 1024)),
                  pl.BlockSpec((1, 8, CONV_CH), lambda b, c: (b, 0, 0)),
                  pl.BlockSpec((1, 1024, M_STATE), lambda b, c: (b, 0, 0)),
                  full((8, CONV_CH)), full((1, CONV_CH)), full((1, 128)), full((1, 1024)), full((1, 128)),
                  full((1, 1024)), full((1, 1024)), full((128, 1024))],
        out_specs=[pl.BlockSpec((1, rows_in, 1024), lambda b, c: (b, c, 0)),
                   pl.BlockSpec((1, 1024, M_STATE), lambda b, c: (b, 0, 0))],
        out_shape=[jax.ShapeDtypeStruct((bsz, t, 1024), BF16),
                   jax.ShapeDtypeStruct((bsz, 1024, M_STATE), F32)],
        scratch_shapes=[pltpu.VMEM((8 + SSD_CHUNK, CONV_CH), F32), pltpu.VMEM((M_STATE, 1024), F32)],
        compiler_params=_cparams(("arbitrary", "arbitrary")),
        name="ssd",
    )(zm3, zm3, zm3, zs3, zm3, prefix8, init, conv_w8, conv_b, dtb, alx, alp, dskx, mnorm, expand)
    return y.reshape(bsz * t, 1024), fin


def _overlap_matrix(n, n_cmp, n_slc, nsp):
    i = np.arange(n)[:, None]
    j = np.arange(nsp)[None, :]
    ov = (i * CMP_STRIDE < j * SLC_BLK + SLC_BLK) & (i * CMP_STRIDE + CMP_LEN > j * SLC_BLK)
    ov = ov & (i < n_cmp) & (j < n_slc)
    return jnp.asarray(ov.astype(np.float32), dtype=BF16)


def _rope_tables(pos):
    inv = ROPE_THETA ** (-jnp.arange(0, HEAD_DIM, 2, dtype=F32) / HEAD_DIM)
    ang = pos.astype(F32)[:, None] * inv[None, :]
    cos, sin = jnp.cos(ang), jnp.sin(ang)
    return jnp.concatenate([cos, cos], axis=1), jnp.concatenate([-sin, sin], axis=1)


def _layer_consts(l, w_in, qk_gain, cmp_pe, cmp_w, pool_w, conv_w, conv_b, dt_bias, a_log, d_skip, mnorm_w,
                  w_branch, w_out):
    offs = np.cumsum((0, 1024, 1536, 24, 1024, 1024, 1024, 1024, CONV_CH, M_HEADS, 3 * D_MODEL))
    seg = lambda k: w_in[l][:, offs[k]:offs[k + 1]]
    xbc = seg(7)
    w_main = jnp.concatenate([seg(0), seg(3), seg(4), seg(5), seg(6), xbc[:, :1024], seg(9), seg(1),
                              xbc[:, 1024:1280], xbc[:, 1280:1536]], axis=1).astype(BF16)
    gates = seg(2)
    zpad = lambda w: jnp.zeros((D_MODEL, w), F32)
    w_small = jnp.concatenate([seg(8), zpad(128 - M_HEADS), gates[:, :12], zpad(116), gates[:, 12:], zpad(116)],
                              axis=1).astype(BF16)
    gain8 = jnp.concatenate([qk_gain[l], jnp.zeros((4, HEAD_DIM), F32)], axis=0)
    w4 = cmp_w[l].reshape(2, 2, 8, 2, HEAD_DIM, HEAD_DIM)
    wcat = jnp.transpose(w4, (0, 2, 3, 4, 1, 5)).reshape(2, 8, 256, 256).astype(BF16)
    pe4 = cmp_pe[l].reshape(2, 2, 8, 2 * HEAD_DIM)
    pe_lhs = jnp.concatenate([jnp.transpose(pe4, (0, 2, 1, 3)), jnp.zeros((2, 8, 14, 256), F32)], axis=2)
    rep = lambda v: jnp.repeat(v, M_HEAD_DIM).reshape(1, 1024)
    pad128 = lambda v: jnp.concatenate([v, jnp.zeros((128 - v.shape[0],), F32)]).reshape(1, 128)
    expand = (np.arange(128)[:, None] == (np.arange(1024)[None, :] // M_HEAD_DIM)).astype(np.float32)
    ssd_consts = (jnp.concatenate([conv_w[l], jnp.zeros((4, CONV_CH), F32)], axis=0), conv_b[l].reshape(1, CONV_CH),
                  pad128(dt_bias[l]), rep(a_log[l]), pad128(a_log[l]), rep(d_skip[l]), mnorm_w[l].reshape(1, 1024),
                  jnp.asarray(expand))
    return dict(w_main=w_main, w_small=w_small, gain8=gain8, wcat=wcat, pe_lhs=pe_lhs,
                pool_w=pool_w[l].astype(BF16), ssd=ssd_consts, w_branch=w_branch[l].astype(BF16),
                w_out=w_out[l].astype(BF16))


def _xbc_cols(z3):
    return jnp.concatenate([z3[..., C_XS:C_XS + 1024], z3[..., C_B:C_B + 256], z3[..., C_C:C_C + 256]], axis=-1)


def _tile(m, pref):
    for t in pref:
        if m % t == 0:
            return t
    return m


def kernel(x_prompt, x_sample, cache_kv, cache_win, state_pool, state_conv, state_ssm, page_table, norm_w, w_in,
           qk_gain, cmp_pe, cmp_w, pool_w, pool_scale, conv_w, conv_b, dt_bias, a_log, d_skip, mnorm_w, w_branch,
           w_out):
    depth = w_in.shape[0]
    bsz, seq, _ = x_prompt.shape
    dec_b, dec_t, _ = x_sample.shape
    n_pool, page = cache_kv.shape[1], cache_kv.shape[2]
    n_pages = page_table.shape[1]
    past_len = n_pages * page
    assert page == 128 and dec_t == 4 and cache_win.shape[2] == WINDOW and seq % 128 == 0
    nq = 8
    cache_pages = cache_kv.reshape(depth, n_pool, page * 4 * N_KV_HEADS, HEAD_DIM)
    cache_win_rows = cache_win.reshape(depth, dec_b, WINDOW * 2 * N_KV_HEADS, HEAD_DIM)
    tk_p = _tile(seq, (256, 128))

    mp = bsz * seq
    ms = dec_b * nq
    xp = x_prompt.reshape(mp, D_MODEL)
    xs = jnp.pad(x_sample, ((0, 0), (0, nq - dec_t), (0, 0))).reshape(ms, D_MODEL)

    cos_p, sin_p = _rope_tables(jnp.tile(jnp.arange(seq), bsz))
    cos_s, sin_s = _rope_tables(jnp.tile(past_len + jnp.arange(nq), dec_b))

    n_p = seq // CMP_STRIDE
    n_slc_p = -(-seq // SLC_BLK)
    nsb_p = -(-n_slc_p // 16) * 16
    assert nsb_p <= 128
    ovt_p = _overlap_matrix(n_p, n_p - 1, n_slc_p, nsb_p).T
    n_s = n_pages * 8
    tk = past_len + dec_t
    n_slc_s = -(-tk // SLC_BLK)
    nsp_s = -(-n_slc_s // 128) * 128
    ov_s = _overlap_matrix(n_s, tk // CMP_STRIDE - 1, n_slc_s, nsp_s)

    tm_p = _tile(mp, (512, 256, 128))
    outs_p = [[] for _ in range(5)]
    outs_s = [[] for _ in range(5)]
    for l in range(depth):
        c = _layer_consts(l, w_in, qk_gain, cmp_pe, cmp_w, pool_w, conv_w, conv_b, dt_bias, a_log, d_skip,
                          mnorm_w, w_branch, w_out)
        h = _rmsnorm(xp, norm_w[l], tm_p)
        z_main = _matmul(h, c["w_main"], tm_p, 1024, "inproj")
        z_small = _matmul(h, c["w_small"], tm_p, N_SMALL, "inproj_small")
        q_r, rows, win, cmpin, kv_att, v_t = _prep(z_main, cos_p, sin_p, c["gain8"], tk_p, True)
        r_cmp, c_cmp = _compress_prompt(cmpin, c["wcat"], c["pe_lhs"], bsz, seq)
        nsa_out = _nsa_prompt(q_r, r_cmp, c_cmp, kv_att, v_t, ovt_p, z_small, z_main, bsz, seq, tk_p)
        pool_out = _pool(z_main, jnp.zeros((bsz, 16, 1024), F32), c["pool_w"], pool_scale[l], bsz, seq,
                         _tile(seq, (256, 128)), 0)
        m_out, fin = _ssd(z_main, z_small, jnp.zeros((bsz, 8, CONV_CH), F32),
                          jnp.zeros((bsz, 1024, M_STATE), F32), c["ssd"], bsz, seq, SSD_CHUNK, seq)
        merged = _merge(nsa_out, pool_out, m_out, c["w_branch"], z_main, tm_p, 512)
        xp_new = _outproj(xp, merged, c["w_out"], tm_p, 512)
        zm3 = z_main.reshape(bsz, seq, N_MAIN)
        outs_p[0].append(rows.reshape(bsz, seq, 4, N_KV_HEADS, HEAD_DIM))
        outs_p[1].append(win.reshape(bsz, seq, 2, N_KV_HEADS, HEAD_DIM)[:, seq - min(WINDOW, seq):])
        outs_p[2].append(zm3[:, seq - POOL_STATE:, C_POOLU:C_POOLU + 1024])
        outs_p[3].append(_xbc_cols(zm3[:, seq - (CONV_W - 1):]))
        outs_p[4].append(fin.reshape(bsz, M_HEADS, M_HEAD_DIM, M_STATE))
        xp = xp_new
        h = _rmsnorm(xs, norm_w[l], ms)
        z_main = _matmul(h, c["w_main"], ms, 1024, "inproj_s")
        z_small = _matmul(h, c["w_small"], ms, N_SMALL, "inproj_small_s")
        q_r, rows, win, _, kv_att = _prep(z_main, cos_s, sin_s, c["gain8"], ms, False)
        new_att = jnp.pad(kv_att.reshape(8, dec_b, nq, HEAD_DIM), ((0, 0), (0, 0), (0, 128 - nq), (0, 0)))
        new_att = jnp.where((jnp.arange(128) < dec_t)[None, None, :, None], new_att, jnp.zeros_like(new_att))
        nsa_out = _nsa_decode(page_table, q_r, cache_pages, c["wcat"], c["pe_lhs"], new_att, cache_win_rows, ov_s,
                              z_small, z_main, l, past_len, nq, dec_t, _tile(n_pages, (16, 8, 4, 2)))
        pre16 = jnp.pad(state_pool[l], ((0, 0), (16 - POOL_STATE, 0), (0, 0)))
        pool_out = _pool(z_main, pre16, c["pool_w"], pool_scale[l], dec_b, nq, nq, past_len)
        pre8 = jnp.pad(state_conv[l], ((0, 0), (8 - (CONV_W - 1), 0), (0, 0)))
        m_out, fin = _ssd(z_main, z_small, pre8, state_ssm[l].reshape(dec_b, 1024, M_STATE), c["ssd"], dec_b, nq,
                          nq, dec_t)
        merged = _merge(nsa_out, pool_out, m_out, c["w_branch"], z_main, ms, 512)
        xs_new = _outproj(xs, merged, c["w_out"], ms, 512)
        zm3 = z_main.reshape(dec_b, nq, N_MAIN)
        outs_s[0].append(rows.reshape(dec_b, nq, 4, N_KV_HEADS, HEAD_DIM)[:, :dec_t])
        win_new = win.reshape(dec_b, nq, 2, N_KV_HEADS, HEAD_DIM)[:, :dec_t]
        outs_s[1].append(jnp.concatenate([cache_win[l], win_new], axis=1)[:, -WINDOW:])
        outs_s[2].append(jnp.concatenate([state_pool[l], zm3[:, :dec_t, C_POOLU:C_POOLU + 1024]],
                                         axis=1)[:, -POOL_STATE:])
        outs_s[3].append(jnp.concatenate([state_conv[l], _xbc_cols(zm3[:, :dec_t])], axis=1)[:, -(CONV_W - 1):])
        outs_s[4].append(fin.reshape(dec_b, M_HEADS, M_HEAD_DIM, M_STATE))
        xs = xs_new

    y_p = xp.reshape(bsz, seq, D_MODEL)
    y_s = xs.reshape(dec_b, nq, D_MODEL)[:, :dec_t]
    stk = lambda lst: jnp.stack(lst)
    return (y_p, y_s, *[stk(v) for v in outs_p], *[stk(v) for v in outs_s])
```

```python
import functools

import numpy as np
import jax
import jax.numpy as jnp
from jax import lax
from jax.experimental import pallas as pl
from jax.experimental.pallas import tpu as pltpu

F32 = jnp.float32
BF16 = jnp.bfloat16

D_MODEL = 2048
BRANCH_WIDTH = D_MODEL // 2
HEAD_DIM = 128
N_HEADS = BRANCH_WIDTH // HEAD_DIM
GQA_REP = 4
N_KV_HEADS = N_HEADS // GQA_REP
ROPE_THETA = 10000.0
CMP_LEN = 32
CMP_STRIDE = 16
SLC_BLK = 64
SLC_SHIFT = SLC_BLK.bit_length() - 1
N_SEL = 16
WINDOW = 512
Q_BLOCK = 128
SEL_BONUS = 1.0e4
POOL_WINDOWS = (2, 4, 8, 16)
POOL_GROUP = BRANCH_WIDTH // len(POOL_WINDOWS)
POOL_STATE = max(POOL_WINDOWS) - 1
M_HEAD_DIM = 64
M_HEADS = BRANCH_WIDTH // M_HEAD_DIM
M_STATE = 128
M_GROUPS = 2
CONV_W = 4
CONV_CH = BRANCH_WIDTH + 2 * M_GROUPS * M_STATE
SSD_CHUNK = 128
EPS = 1e-6
SCALE = HEAD_DIM ** -0.5
NEG_BIG = -1.0e30

C_Q = 0
C_NSAZ = 1024
C_POOLU = 2048
C_POOLZ = 3072
C_MZ = 4096
C_XS = 5120
C_MG = 6144
C_KV = 12288
C_B = 13824
C_C = 14080
N_MAIN = 14336
N_SMALL = 384

VMEM_LIMIT = 56 * 1024 * 1024
HI = lax.Precision.HIGHEST


def _cparams(sem):
    return pltpu.CompilerParams(dimension_semantics=sem, vmem_limit_bytes=VMEM_LIMIT)


def _nt(a, b):
    return lax.dot_general(a, b, (((1,), (1,)), ((), ())), preferred_element_type=F32)


def _silu(x):
    return x * jax.nn.sigmoid(x)


def _rmsnorm_kernel(x_ref, g_ref, o_ref):
    x = x_ref[...]
    ms = jnp.mean(x * x, axis=-1, keepdims=True)
    o_ref[...] = (x * lax.rsqrt(ms + EPS) * g_ref[...]).astype(o_ref.dtype)


def _rmsnorm(x, g, tr):
    m, d = x.shape
    return pl.pallas_call(
        _rmsnorm_kernel,
        grid=(m // tr,),
        in_specs=[pl.BlockSpec((tr, d), lambda i: (i, 0)), pl.BlockSpec((1, d), lambda i: (0, 0))],
        out_specs=pl.BlockSpec((tr, d), lambda i: (i, 0)),
        out_shape=jax.ShapeDtypeStruct((m, d), BF16),
        compiler_params=_cparams(("parallel",)),
        name="rmsnorm",
    )(x, g.reshape(1, d))


def _mm_kernel(a_ref, b_ref, o_ref):
    o_ref[...] = jnp.dot(a_ref[...], b_ref[...], preferred_element_type=F32).astype(o_ref.dtype)


def _matmul(a, b, tm, tn, name):
    m, k = a.shape
    n = b.shape[1]
    return pl.pallas_call(
        _mm_kernel,
        grid=(m // tm, n // tn),
        in_specs=[pl.BlockSpec((tm, k), lambda i, j: (i, 0)), pl.BlockSpec((k, tn), lambda i, j: (0, j))],
        out_specs=pl.BlockSpec((tm, tn), lambda i, j: (i, j)),
        out_shape=jax.ShapeDtypeStruct((m, n), F32),
        compiler_params=_cparams(("parallel", "parallel")),
        name=name,
    )(a, b)


def _merge_kernel(a_ref, p_ref, m_ref, w_ref, g0_ref, g1_ref, g2_ref, o_ref):
    acc = jax.nn.sigmoid(g0_ref[...]) * jnp.dot(a_ref[...], w_ref[0], preferred_element_type=F32)
    acc = acc + jax.nn.sigmoid(g1_ref[...]) * jnp.dot(p_ref[...], w_ref[1], preferred_element_type=F32)
    acc = acc + jax.nn.sigmoid(g2_ref[...]) * jnp.dot(m_ref[...], w_ref[2], preferred_element_type=F32)
    o_ref[...] = acc.astype(o_ref.dtype)


def _merge(nsa_out, pool_out, m_out, w_branch, z_main, tm, tn):
    m = nsa_out.shape[0]
    gblk = [(C_MG + b * D_MODEL) // tn for b in range(3)]
    act = pl.BlockSpec((tm, BRANCH_WIDTH), lambda i, j: (i, 0))
    return pl.pallas_call(
        _merge_kernel,
        grid=(m // tm, D_MODEL // tn),
        in_specs=[act, act, act,
                  pl.BlockSpec((3, BRANCH_WIDTH, tn), lambda i, j: (0, 0, j)),
                  pl.BlockSpec((tm, tn), lambda i, j: (i, gblk[0] + j)),
                  pl.BlockSpec((tm, tn), lambda i, j: (i, gblk[1] + j)),
                  pl.BlockSpec((tm, tn), lambda i, j: (i, gblk[2] + j))],
        out_specs=pl.BlockSpec((tm, tn), lambda i, j: (i, j)),
        out_shape=jax.ShapeDtypeStruct((m, D_MODEL), BF16),
        compiler_params=_cparams(("parallel", "parallel")),
        name="merge",
    )(nsa_out, pool_out, m_out, w_branch, z_main, z_main, z_main)


def _outproj_kernel(x_ref, a_ref, w_ref, g_ref, o_ref, h_ref):
    y = x_ref[...] + jnp.dot(a_ref[...], w_ref[...], preferred_element_type=F32)
    o_ref[...] = y
    ms = jnp.mean(y * y, axis=-1, keepdims=True)
    h_ref[...] = (y * lax.rsqrt(ms + EPS) * g_ref[...]).astype(h_ref.dtype)


def _outproj(x, merged, w_out, g_next, tm):
    m = x.shape[0]
    row = pl.BlockSpec((tm, D_MODEL), lambda i: (i, 0))
    return pl.pallas_call(
        _outproj_kernel,
        grid=(m // tm,),
        in_specs=[row, row, pl.BlockSpec((D_MODEL, D_MODEL), lambda i: (0, 0)),
                  pl.BlockSpec((1, D_MODEL), lambda i: (0, 0))],
        out_specs=[row, row],
        out_shape=[jax.ShapeDtypeStruct((m, D_MODEL), F32), jax.ShapeDtypeStruct((m, D_MODEL), BF16)],
        compiler_params=_cparams(("parallel",)),
        name="outproj",
    )(x, merged, w_out, g_next.reshape(1, D_MODEL))


def _prep_kernel(zq_ref, zkv_ref, cos_ref, sin_ref, gain_ref, rows_in_ref, win_in_ref,
                 q_ref, rows_ref, win_ref, cmpin_ref, kvatt_ref, *maybe_vt_ref, tr):
    del rows_in_ref, win_in_ref
    cos = cos_ref[...]
    sin = sin_ref[...]

    def norm_rope(x, gi):
        ms = jnp.mean(x * x, axis=-1, keepdims=True)
        y = x * lax.rsqrt(ms + EPS) * gain_ref[gi:gi + 1, :]
        return y * cos + pltpu.roll(y, HEAD_DIM // 2, 1) * sin

    for h in range(N_HEADS):
        sl = slice(h * HEAD_DIM, (h + 1) * HEAD_DIM)
        q_ref[:, sl] = norm_rope(zq_ref[:, sl], 0).astype(q_ref.dtype)
    for br in range(3):
        for g in range(N_KV_HEADS):
            ck = br * 512 + g * HEAD_DIM
            cv = br * 512 + 256 + g * HEAD_DIM
            k = norm_rope(zkv_ref[:, ck:ck + HEAD_DIM], 1 + br)
            v = zkv_ref[:, cv:cv + HEAD_DIM]
            if br < 2:
                rows_ref[0, pl.ds(br * 4 + g, tr, stride=8), :] = k
                rows_ref[0, pl.ds(br * 4 + 2 + g, tr, stride=8), :] = v
            else:
                win_ref[0, pl.ds(g, tr, stride=4), :] = k
                win_ref[0, pl.ds(2 + g, tr, stride=4), :] = v
            if br == 0:
                cmpin_ref[g] = k
                cmpin_ref[2 + g] = v
            else:
                base = (br - 1) * 4
                kvatt_ref[base + g] = k.astype(kvatt_ref.dtype)
                kvatt_ref[base + 2 + g] = v.astype(kvatt_ref.dtype)
                if maybe_vt_ref:
                    for j in range(tr // 128):
                        maybe_vt_ref[0][(br - 1) * 2 + g, j] = v[j * 128:(j + 1) * 128].T.astype(kvatt_ref.dtype)


def _prep(z_main, cos2, sin2, gain8, tr, emit_vt, layer, depth, rows_all, win_all):
    m = z_main.shape[0]
    vt_specs = [pl.BlockSpec((4, tr // 128, HEAD_DIM, 128), lambda i: (0, i, 0, 0))] if emit_vt else []
    vt_shapes = [jax.ShapeDtypeStruct((4, m // 128, HEAD_DIM, 128), BF16)] if emit_vt else []
    return pl.pallas_call(
        functools.partial(_prep_kernel, tr=tr),
        grid=(m // tr,),
        in_specs=[pl.BlockSpec((tr, 1024), lambda i: (i, C_Q // 1024)),
                  pl.BlockSpec((tr, 1536), lambda i: (i, C_KV // 1536)),
                  pl.BlockSpec((tr, HEAD_DIM), lambda i: (i, 0)),
                  pl.BlockSpec((tr, HEAD_DIM), lambda i: (i, 0)),
                  pl.BlockSpec((8, HEAD_DIM), lambda i: (0, 0)),
                  pl.BlockSpec(memory_space=pl.ANY), pl.BlockSpec(memory_space=pl.ANY)],
        out_specs=[pl.BlockSpec((tr, 1024), lambda i: (i, 0)),
                   pl.BlockSpec((1, tr * 8, HEAD_DIM), lambda i: (layer, i, 0)),
                   pl.BlockSpec((1, tr * 4, HEAD_DIM), lambda i: (layer, i, 0)),
                   pl.BlockSpec((4, tr, HEAD_DIM), lambda i: (0, i, 0)),
                   pl.BlockSpec((8, tr, HEAD_DIM), lambda i: (0, i, 0))] + vt_specs,
        out_shape=[jax.ShapeDtypeStruct((m, 1024), BF16),
                   jax.ShapeDtypeStruct((depth, m * 8, HEAD_DIM), F32),
                   jax.ShapeDtypeStruct((depth, m * 4, HEAD_DIM), F32),
                   jax.ShapeDtypeStruct((4, m, HEAD_DIM), F32),
                   jax.ShapeDtypeStruct((8, m, HEAD_DIM), BF16)] + vt_shapes,
        input_output_aliases={5: 1, 6: 2},
        compiler_params=_cparams(("parallel",)),
        name="nsa_prep",
    )(z_main, z_main, cos2, sin2, gain8, rows_all, win_all)


def _compress_rows(load_pair, w_ref, c, n):
    acc = jnp.zeros((n, 256), F32)
    for j in range(8):
        xa, xb = load_pair(j)
        lhs = jnp.concatenate([xa, xb], axis=1).astype(BF16)
        acc = acc + jnp.dot(lhs, w_ref[c, j], preferred_element_type=F32)
    return acc


def _compress_const(pe_ref, w_ref, c):
    acc = jnp.zeros((16, 256), F32)
    for j in range(8):
        acc = acc + jnp.dot(pe_ref[c, j].astype(BF16), w_ref[c, j], preferred_element_type=F32)
    return acc[0:1, 0:128] + acc[1:2, 128:256]


def _compress_prompt_kernel(x_ref, w_ref, pe_ref, r_ref, c_ref, *, n):
    c = pl.program_id(1) // 2

    def load_pair(j):
        return (x_ref[0, 0, pl.ds(2 * j, n, stride=16), :],
                x_ref[0, 0, pl.ds(2 * j + 1, n, stride=16), :])

    r_ref[0, 0, 0:n, :] = _compress_rows(load_pair, w_ref, c, n)
    r_ref[0, 0, n:n + 8, :] = jnp.zeros((8, 256), F32)
    c_ref[0, 0] = jnp.broadcast_to(_compress_const(pe_ref, w_ref, c), (8, 128))


def _compress_prompt(cmpin, wcat, pe_lhs, bsz, t):
    n = t // CMP_STRIDE
    x = cmpin.reshape(4, bsz, t, HEAD_DIM)
    return pl.pallas_call(
        functools.partial(_compress_prompt_kernel, n=n),
        grid=(bsz, 4),
        in_specs=[pl.BlockSpec((1, 1, t, HEAD_DIM), lambda b, cg: (cg, b, 0, 0)),
                  pl.BlockSpec((2, 8, 256, 256), lambda b, cg: (0, 0, 0, 0)),
                  pl.BlockSpec((2, 8, 16, 256), lambda b, cg: (0, 0, 0, 0))],
        out_specs=[pl.BlockSpec((1, 1, n + 8, 256), lambda b, cg: (b, cg, 0, 0)),
                   pl.BlockSpec((1, 1, 8, 128), lambda b, cg: (b, cg, 0, 0))],
        out_shape=[jax.ShapeDtypeStruct((bsz, 4, n + 8, 256), F32),
                   jax.ShapeDtypeStruct((bsz, 4, 8, 128), F32)],
        compiler_params=_cparams(("parallel", "parallel")),
        name="compress_prompt",
    )(x, wcat, pe_lhs)


def _combine_cmp(r, cc, n, n_cmp):
    a = r[0:n, 0:128]
    bsh = r[1:n + 1, 128:256]
    ci = lax.broadcasted_iota(jnp.int32, (n, 1), 0)
    return jnp.where(ci < n_cmp, a + bsh + cc, 0.0)


def _softmax_rows(s, mask):
    s = jnp.where(mask, s, -jnp.inf)
    m = jnp.max(s, axis=-1, keepdims=True)
    m = jnp.where(m > -jnp.inf, m, 0.0)
    e = jnp.exp(s - m)
    return e / jnp.maximum(jnp.sum(e, axis=-1, keepdims=True), 1e-30)


def _cmp_and_select(q4, kc, vc, ov, t_q, n, n_cmp, n_slc, n_sel, nsp):
    nq = t_q.shape[0]
    t_rows = jnp.concatenate([t_q] * GQA_REP, axis=0)
    sc = _nt(q4, kc) * SCALE
    ci = lax.broadcasted_iota(jnp.int32, (1, n), 1)
    cmask = (ci * CMP_STRIDE + (CMP_LEN - 1) <= t_rows) & (ci < n_cmp)
    pc = _softmax_rows(sc, cmask)
    o_cmp = jnp.dot(pc.astype(BF16), vc, preferred_element_type=F32)
    pcs = pc[0:nq] + pc[nq:2 * nq] + pc[2 * nq:3 * nq] + pc[3 * nq:4 * nq]
    hi = pcs.astype(BF16)
    lo = (pcs - hi.astype(F32)).astype(BF16)
    imp = jnp.dot(hi, ov, preferred_element_type=F32) + jnp.dot(lo, ov, preferred_element_type=F32)
    j = lax.broadcasted_iota(jnp.int32, (1, nsp), 1)
    cur = t_q >> SLC_SHIFT
    forced = (j == 0) | (j == cur) | (j == cur - 1)
    valid = (j * SLC_BLK <= t_q) & (j < n_slc)
    score = jnp.where(valid, imp + SEL_BONUS * forced.astype(F32), -jnp.inf)
    rank = jnp.zeros((nq, nsp), F32)
    for k in range(n_slc):
        col = score[:, k:k + 1]
        before = (col > score) | ((col == score) & (k < j))
        rank = rank + before.astype(F32)
    sel = (rank < float(n_sel)) & (j < n_slc)
    return o_cmp, sel.astype(BF16)


def _sel_mask(sel_win, win0, blk0, t_q, kpos0, nk):
    jb = win0 + lax.broadcasted_iota(jnp.int32, (128, nk), 0)
    kk = lax.broadcasted_iota(jnp.int32, (128, nk), 1)
    expand = (jb == blk0 + (kk >> SLC_SHIFT)).astype(BF16)
    hit = jnp.dot(sel_win, expand, preferred_element_type=F32)
    kpos = kpos0 + lax.broadcasted_iota(jnp.int32, (1, nk), 1)
    return (hit > 0.5) & (kpos <= t_q)


def _gate_and_write(o_ref_write, gts, nz, o_cmp, o_slc, o_win, nq):
    for r in range(GQA_REP):
        rs = slice(r * nq, (r + 1) * nq)
        o = (gts[:, 3 * r:3 * r + 1] * o_cmp[rs] + gts[:, 3 * r + 1:3 * r + 2] * o_slc[rs]
             + gts[:, 3 * r + 2:3 * r + 3] * o_win[rs])
        o_ref_write(r, o * _silu(nz[:, r * HEAD_DIM:(r + 1) * HEAD_DIM]))


def _nsa_prompt_kernel(q_ref, rk_ref, rv_ref, ck_ref, cv_ref, ks_ref, vst_ref, kw_ref, vwt_ref, ovt_ref,
                       gz_ref, nz_ref, o_ref, *, n, n_cmp, n_slc, n_sel, nsb, tk, nwt):
    i = pl.program_id(2)
    nq = Q_BLOCK
    nr = GQA_REP * nq
    qb = q_ref[0]
    q4 = jnp.concatenate([qb[:, r * HEAD_DIM:(r + 1) * HEAD_DIM] for r in range(GQA_REP)], axis=0)
    t_q = i * nq + lax.broadcasted_iota(jnp.int32, (1, nq), 1)
    t_q4 = jnp.concatenate([t_q] * GQA_REP, axis=1)

    kc = _combine_cmp(rk_ref.at[0, 0], ck_ref[0, 0, 0:1, :], n, n_cmp).astype(BF16)
    vct = _combine_cmp(rv_ref.at[0, 0], cv_ref[0, 0, 0:1, :], n, n_cmp).T.astype(BF16)
    sc = _nt(kc, q4) * SCALE
    ci = lax.broadcasted_iota(jnp.int32, (n, 1), 0)
    cmask = (ci * CMP_STRIDE + (CMP_LEN - 1) <= t_q4) & (ci < n_cmp)
    sc = jnp.where(cmask, sc, -jnp.inf)
    mx = jnp.max(sc, axis=0, keepdims=True)
    mx = jnp.where(mx > -jnp.inf, mx, 0.0)
    e = jnp.exp(sc - mx)
    pc = e * (1.0 / jnp.maximum(jnp.sum(e, axis=0, keepdims=True), 1e-30))
    o_cmp = jnp.dot(vct, pc.astype(BF16), preferred_element_type=F32)
    pcs = pc[:, 0:nq] + pc[:, nq:2 * nq] + pc[:, 2 * nq:3 * nq] + pc[:, 3 * nq:4 * nq]
    hi = pcs.astype(BF16)
    lo = (pcs - hi.astype(F32)).astype(BF16)
    ovt = ovt_ref[...]
    imp = jnp.dot(ovt, hi, preferred_element_type=F32) + jnp.dot(ovt, lo, preferred_element_type=F32)

    jb = lax.broadcasted_iota(jnp.int32, (nsb, 1), 0)
    cur = t_q >> SLC_SHIFT
    forced = (jb == 0) | (jb == cur) | (jb == cur - 1)
    valid = (jb * SLC_BLK <= t_q) & (jb < n_slc)
    score = jnp.where(valid, imp + SEL_BONUS * forced.astype(F32), -jnp.inf)
    rank = jnp.zeros((nsb, nq), F32)
    for k in range(n_slc):
        row = score[k:k + 1, :]
        before = (row > score) | ((row == score) & (k < jb))
        rank = rank + before.astype(F32)
    sel = ((rank < float(n_sel)) & (jb < n_slc)).astype(BF16)
    if nsb < 128:
        sel = jnp.concatenate([sel, jnp.zeros((128 - nsb, nq), BF16)], axis=0)

    def masked_scores(k, mq):
        st = _nt(k, q4) * SCALE
        return jnp.concatenate([jnp.where(mq, st[:, r * nq:(r + 1) * nq], -jnp.inf)
                                for r in range(GQA_REP)], axis=1)

    def keys_at(k_ref, vt_ref, tile0, ntiles):
        k = k_ref[0, 0, pl.ds(pl.multiple_of(tile0 * 128, 128), ntiles * 128), :]
        vt = jnp.concatenate([vt_ref[0, 0, tile0 + j] for j in range(ntiles)], axis=1)
        kpos = tile0 * 128 + lax.broadcasted_iota(jnp.int32, (ntiles * 128, 1), 0)
        return k, vt, kpos

    tpt = tk // 128

    def slc_body(kb, carry):
        m, l, acc = carry
        k, vt, kpos = keys_at(ks_ref, vst_ref, kb * tpt, tpt)
        jl = lax.broadcasted_iota(jnp.int32, (tk, 128), 1)
        expand = (jl == (kpos >> SLC_SHIFT)).astype(BF16)
        hit = jnp.dot(expand, sel, preferred_element_type=F32)
        st = masked_scores(k, (hit > 0.5) & (kpos <= t_q))
        m_new = jnp.maximum(m, jnp.max(st, axis=0, keepdims=True))
        p = jnp.exp(st - m_new)
        alpha = jnp.exp(m - m_new)
        l_new = alpha * l + jnp.sum(p, axis=0, keepdims=True)
        acc_new = alpha * acc + jnp.dot(vt, p.astype(BF16), preferred_element_type=F32)
        return m_new, l_new, acc_new

    init = (jnp.full((1, nr), NEG_BIG, F32), jnp.zeros((1, nr), F32), jnp.zeros((HEAD_DIM, nr), F32))
    m, l, acc = lax.fori_loop(0, (i * nq + nq - 1) // tk + 1, slc_body, init)
    o_slc = acc * (1.0 / jnp.maximum(l, 1e-30))

    k, vt, kpos = keys_at(kw_ref, vwt_ref, jnp.maximum(i + 1 - nwt, 0), nwt)
    d = t_q - kpos
    st = masked_scores(k, (d >= 0) & (d < WINDOW))
    mw = jnp.max(st, axis=0, keepdims=True)
    mw = jnp.where(mw > -jnp.inf, mw, 0.0)
    p = jnp.exp(st - mw)
    o_win = (jnp.dot(vt, p.astype(BF16), preferred_element_type=F32)
             * (1.0 / jnp.maximum(jnp.sum(p, axis=0, keepdims=True), 1e-30)))

    gt = jax.nn.sigmoid(gz_ref[0]).T
    nz = nz_ref[0]
    for r in range(GQA_REP):
        cs = slice(r * nq, (r + 1) * nq)
        ot = (gt[3 * r:3 * r + 1, :] * o_cmp[:, cs] + gt[3 * r + 1:3 * r + 2, :] * o_slc[:, cs]
              + gt[3 * r + 2:3 * r + 3, :] * o_win[:, cs])
        hs = slice(r * HEAD_DIM, (r + 1) * HEAD_DIM)
        o_ref[0, :, hs] = (ot.T * _silu(nz[:, hs])).astype(o_ref.dtype)


def _nsa_prompt(q_r, r_cmp, c_cmp, kv_att, v_t, ovt, z_small, z_main, bsz, t, tk):
    n = t // CMP_STRIDE
    n_cmp = n - CMP_LEN // CMP_STRIDE + 1
    n_slc = -(-t // SLC_BLK)
    nsb = ovt.shape[0]
    q3 = q_r.reshape(bsz, t, 1024)
    kv4 = kv_att.reshape(8, bsz, t, HEAD_DIM)
    vt5 = v_t.reshape(4, bsz, t // 128, HEAD_DIM, 128)
    nwt = min(WINDOW // Q_BLOCK + 1, t // 128)
    zs3 = z_small.reshape(bsz, t, N_SMALL)
    zm3 = z_main.reshape(bsz, t, N_MAIN)
    rspec = lambda off: pl.BlockSpec((1, 1, n + 8, 256), lambda b, g, i: (b, off + g, 0, 0))
    cspec = lambda off: pl.BlockSpec((1, 1, 8, 128), lambda b, g, i: (b, off + g, 0, 0))
    kspec = lambda off: pl.BlockSpec((1, 1, t, HEAD_DIM), lambda b, g, i: (off + g, b, 0, 0))
    vtspec = lambda off: pl.BlockSpec((1, 1, t // 128, HEAD_DIM, 128), lambda b, g, i: (off + g, b, 0, 0, 0))
    out = pl.pallas_call(
        functools.partial(_nsa_prompt_kernel, n=n, n_cmp=n_cmp, n_slc=n_slc, n_sel=min(N_SEL, n_slc), nsb=nsb,
                          tk=tk, nwt=nwt),
        grid=(bsz, N_KV_HEADS, t // Q_BLOCK),
        in_specs=[pl.BlockSpec((1, Q_BLOCK, 512), lambda b, g, i: (b, i, g)),
                  rspec(0), rspec(2), cspec(0), cspec(2),
                  kspec(0), vtspec(0), kspec(4), vtspec(2),
                  pl.BlockSpec(ovt.shape, lambda b, g, i: (0, 0)),
                  pl.BlockSpec((1, Q_BLOCK, 128), lambda b, g, i: (b, i, 1 + g)),
                  pl.BlockSpec((1, Q_BLOCK, 512), lambda b, g, i: (b, i, C_NSAZ // 512 + g))],
        out_specs=pl.BlockSpec((1, Q_BLOCK, 512), lambda b, g, i: (b, i, g)),
        out_shape=jax.ShapeDtypeStruct((bsz, t, 1024), BF16),
        compiler_params=_cparams(("parallel", "parallel", "arbitrary")),
        name="nsa_prompt",
    )(q3, r_cmp, r_cmp, c_cmp, c_cmp, kv4, vt5, kv4, vt5, ovt, zs3, zm3)
    return out.reshape(bsz * t, 1024)


def _nsa_decode_kernel(pt_ref, cache_ref, q_ref, w_ref, pe_ref, new_ref, cw_ref, ov_ref, zs_ref, nz_ref, o_ref,
                       buf, sem, r_scr, sc_scr, v_scr,
                       *, layer, pp, ns, n, n_cmp, n_slc, n_sel, nsp, past_len, nq):
    b = pl.program_id(0)
    s = pl.program_id(1)
    step = b * ns + s
    nsteps = pl.num_programs(0) * ns
    slot = step % 2
    nch = pp * 8
    nk = pp * 128
    nr = GQA_REP * nq

    def copies(bb, ss, sl):
        return [pltpu.make_async_copy(cache_ref.at[layer, pt_ref[bb, ss * pp + j]],
                                      buf.at[sl, pl.ds(j * 1024, 1024), :], sem.at[sl]) for j in range(pp)]

    @pl.when(step == 0)
    def _():
        for cp in copies(b, s, slot):
            cp.start()

    @pl.when(step + 1 < nsteps)
    def _():
        nxt = step + 1
        for cp in copies(nxt // ns, nxt % ns, 1 - slot):
            cp.start()

    for cp in copies(b, s, slot):
        cp.wait()

    qb = q_ref[0]
    q4s = [jnp.concatenate([qb[:, (g * GQA_REP + r) * HEAD_DIM:(g * GQA_REP + r + 1) * HEAD_DIM]
                            for r in range(GQA_REP)], axis=0) for g in range(N_KV_HEADS)]

    @pl.when(s == 0)
    def _():
        r_scr[:, ns * nch:ns * nch + 8, :] = jnp.zeros((4, 8, 256), F32)

    row0 = pl.multiple_of(s * nch, nch)
    for c in range(2):
        acc = jnp.zeros((N_KV_HEADS * nch, 256), F32)
        for j in range(8):
            def chunk_rows(l, g):
                return buf[slot, pl.ds(l * 8 + c * 2 + g, nch, stride=128), :]
            lhs = jnp.concatenate(
                [jnp.concatenate([chunk_rows(2 * j, g), chunk_rows(2 * j + 1, g)], axis=1)
                 for g in range(N_KV_HEADS)], axis=0).astype(BF16)
            acc = acc + jnp.dot(lhs, w_ref[c, j], preferred_element_type=F32)
        for g in range(N_KV_HEADS):
            r_scr[c * 2 + g, pl.ds(row0, nch), :] = acc[g * nch:(g + 1) * nch]

    key0 = pl.multiple_of(s * nk, nk)
    for g in range(N_KV_HEADS):
        k = buf[slot, pl.ds(4 + g, nk, stride=8), :].astype(BF16)
        sc_scr[g, s] = _nt(q4s[g], k) * SCALE
        v_scr[g, pl.ds(key0, nk), :] = buf[slot, pl.ds(6 + g, nk, stride=8), :].astype(BF16)

    @pl.when(s == ns - 1)
    def _():
        t_q = past_len + lax.broadcasted_iota(jnp.int32, (nq, 1), 0)
        t_rows = jnp.concatenate([t_q] * GQA_REP, axis=0)
        gts_all = jax.nn.sigmoid(zs_ref[0])
        nz_all = nz_ref[0]
        cck = _compress_const(pe_ref, w_ref, 0)
        ccv = _compress_const(pe_ref, w_ref, 1)
        for g in range(N_KV_HEADS):
            q4 = q4s[g]
            kc = _combine_cmp(r_scr.at[g], cck, n, n_cmp).astype(BF16)
            vc = _combine_cmp(r_scr.at[2 + g], ccv, n, n_cmp).astype(BF16)
            o_cmp, sel = _cmp_and_select(q4, kc, vc, ov_ref[...], t_q, n, n_cmp, n_slc, n_sel, nsp)

            def chunk_mask(blk0, kpos0, nkeys):
                win0 = (blk0 // 128) * 128
                mq = _sel_mask(sel[:, win0:win0 + 128], win0, blk0, t_q, kpos0, nkeys)
                return jnp.concatenate([mq] * GQA_REP, axis=0)

            mx = jnp.full((nr, 1), -jnp.inf, F32)
            for cidx in range(ns):
                sm = jnp.where(chunk_mask(cidx * pp * 2, cidx * nk, nk), sc_scr[g, cidx], -jnp.inf)
                sc_scr[g, cidx] = sm
                mx = jnp.maximum(mx, jnp.max(sm, axis=-1, keepdims=True))
            s_new = jnp.where(chunk_mask(past_len // SLC_BLK, past_len, 128),
                              _nt(q4, new_ref[g, 0]) * SCALE, -jnp.inf)
            mx = jnp.maximum(mx, jnp.max(s_new, axis=-1, keepdims=True))
            mx = jnp.where(mx > -jnp.inf, mx, 0.0)
            p_new = jnp.exp(s_new - mx)
            den = jnp.sum(p_new, axis=-1, keepdims=True)
            acc = jnp.dot(p_new.astype(BF16), new_ref[2 + g, 0], preferred_element_type=F32)
            for cidx in range(ns):
                p = jnp.exp(sc_scr[g, cidx] - mx)
                den = den + jnp.sum(p, axis=-1, keepdims=True)
                acc = acc + jnp.dot(p.astype(BF16), v_scr[g, cidx * nk:(cidx + 1) * nk, :],
                                    preferred_element_type=F32)
            o_slc = acc / jnp.maximum(den, 1e-30)

            kw = cw_ref[0, 0, pl.ds(g, WINDOW, stride=4), :].astype(BF16)
            vw = cw_ref[0, 0, pl.ds(2 + g, WINDOW, stride=4), :].astype(BF16)
            d1 = t_rows - (past_len - WINDOW + lax.broadcasted_iota(jnp.int32, (1, WINDOW), 1))
            s1 = jnp.where((d1 >= 0) & (d1 < WINDOW), _nt(q4, kw) * SCALE, -jnp.inf)
            d2 = t_rows - (past_len + lax.broadcasted_iota(jnp.int32, (1, 128), 1))
            s2 = jnp.where((d2 >= 0) & (d2 < WINDOW), _nt(q4, new_ref[4 + g, 0]) * SCALE, -jnp.inf)
            mw = jnp.maximum(jnp.max(s1, axis=-1, keepdims=True), jnp.max(s2, axis=-1, keepdims=True))
            mw = jnp.where(mw > -jnp.inf, mw, 0.0)
            p1 = jnp.exp(s1 - mw)
            p2 = jnp.exp(s2 - mw)
            denw = jnp.sum(p1, axis=-1, keepdims=True) + jnp.sum(p2, axis=-1, keepdims=True)
            o_win = (jnp.dot(p1.astype(BF16), vw, preferred_element_type=F32)
                     + jnp.dot(p2.astype(BF16), new_ref[6 + g, 0], preferred_element_type=F32))
            o_win = o_win / jnp.maximum(denw, 1e-30)

            def write(r, val, g=g):
                c0 = (g * GQA_REP + r) * HEAD_DIM
                o_ref[0, :, c0:c0 + HEAD_DIM] = val.astype(o_ref.dtype)

            _gate_and_write(write, gts_all[:, 128 * (1 + g):128 * (2 + g)],
                            nz_all[:, g * 512:(g + 1) * 512], o_cmp, o_slc, o_win, nq)


def _nsa_decode(page_table, q_s, cache_pages, wcat, pe_lhs, new_att, cache_win_rows, ov, z_small, z_main, layer,
                past_len, nq, dec_t, pp):
    dec_b, n_pages = page_table.shape
    ns = n_pages // pp
    n = n_pages * 8
    tk = past_len + dec_t
    n_cmp = tk // CMP_STRIDE - CMP_LEN // CMP_STRIDE + 1
    n_slc = -(-tk // SLC_BLK)
    nsp = ov.shape[1]
    nr = GQA_REP * nq
    q3 = q_s.reshape(dec_b, nq, 1024)
    zs3 = z_small.reshape(dec_b, nq, N_SMALL)
    zm3 = z_main.reshape(dec_b, nq, N_MAIN)
    grid_spec = pltpu.PrefetchScalarGridSpec(
        num_scalar_prefetch=1,
        grid=(dec_b, ns),
        in_specs=[pl.BlockSpec(memory_space=pl.ANY),
                  pl.BlockSpec((1, nq, 1024), lambda b, s, pt: (b, 0, 0)),
                  pl.BlockSpec((2, 8, 256, 256), lambda b, s, pt: (0, 0, 0, 0)),
                  pl.BlockSpec((2, 8, 16, 256), lambda b, s, pt: (0, 0, 0, 0)),
                  pl.BlockSpec((8, 1, 128, HEAD_DIM), lambda b, s, pt: (0, b, 0, 0)),
                  pl.BlockSpec((1, 1, 4 * WINDOW, HEAD_DIM), lambda b, s, pt: (layer, b, 0, 0)),
                  pl.BlockSpec(ov.shape, lambda b, s, pt: (0, 0)),
                  pl.BlockSpec((1, nq, N_SMALL), lambda b, s, pt: (b, 0, 0)),
                  pl.BlockSpec((1, nq, 1024), lambda b, s, pt: (b, 0, C_NSAZ // 1024))],
        out_specs=pl.BlockSpec((1, nq, 1024), lambda b, s, pt: (b, 0, 0)),
        scratch_shapes=[pltpu.VMEM((2, pp * 1024, HEAD_DIM), F32),
                        pltpu.SemaphoreType.DMA((2,)),
                        pltpu.VMEM((4, n + 8, 256), F32),
                        pltpu.VMEM((N_KV_HEADS, ns, nr, pp * 128), F32),
                        pltpu.VMEM((N_KV_HEADS, n_pages * 128, HEAD_DIM), BF16)],
    )
    out = pl.pallas_call(
        functools.partial(_nsa_decode_kernel, layer=layer, pp=pp, ns=ns, n=n, n_cmp=n_cmp, n_slc=n_slc,
                          n_sel=min(N_SEL, n_slc), nsp=nsp, past_len=past_len, nq=nq),
        grid_spec=grid_spec,
        out_shape=jax.ShapeDtypeStruct((dec_b, nq, 1024), BF16),
        compiler_params=_cparams(("arbitrary", "arbitrary")),
        name="nsa_decode",
    )(page_table, cache_pages, q3, wcat, pe_lhs, new_att, cache_win_rows, ov, zs3, zm3)
    return out.reshape(dec_b * nq, 1024)


def _pool_kernel(u_ref, pz_ref, pre_ref, w_ref, sc_ref, o_ref, buf, *, tr, pos0):
    i = pl.program_id(1)

    @pl.when(i == 0)
    def _():
        buf[0:16, :] = pre_ref[0]

    u = u_ref[0]
    buf[16:16 + tr, :] = u
    pos = pos0 + i * tr + lax.broadcasted_iota(jnp.int32, (tr, 1), 0)
    for gi, w in enumerate(POOL_WINDOWS):
        cs = slice(gi * POOL_GROUP, (gi + 1) * POOL_GROUP)
        s = buf[16:16 + tr, cs]
        for k in range(1, w):
            s = s + buf[16 - k:16 - k + tr, cs]
        cnt = jnp.minimum(w, pos + 1).astype(F32)
        pooled = s / cnt - u[:, cs]
        mixed = jnp.dot(pooled.astype(BF16), w_ref[gi], preferred_element_type=F32) * sc_ref[:, cs]
        o_ref[0, :, cs] = (mixed * _silu(pz_ref[0, :, cs])).astype(o_ref.dtype)
    buf[0:16, :] = buf[tr:tr + 16, :]


def _pool(z_main, prefix16, pool_w, pool_scale, bsz, t, tr, pos0):
    zm3 = z_main.reshape(bsz, t, N_MAIN)
    out = pl.pallas_call(
        functools.partial(_pool_kernel, tr=tr, pos0=pos0),
        grid=(bsz, t // tr),
        in_specs=[pl.BlockSpec((1, tr, 1024), lambda b, i: (b, i, C_POOLU // 1024)),
                  pl.BlockSpec((1, tr, 1024), lambda b, i: (b, i, C_POOLZ // 1024)),
                  pl.BlockSpec((1, 16, 1024), lambda b, i: (b, 0, 0)),
                  pl.BlockSpec((4, POOL_GROUP, POOL_GROUP), lambda b, i: (0, 0, 0)),
                  pl.BlockSpec((1, 1024), lambda b, i: (0, 0))],
        out_specs=pl.BlockSpec((1, tr, 1024), lambda b, i: (b, i, 0)),
        out_shape=jax.ShapeDtypeStruct((bsz, t, 1024), BF16),
        scratch_shapes=[pltpu.VMEM((16 + tr, 1024), F32)],
        compiler_params=_cparams(("arbitrary", "arbitrary")),
        name="pool",
    )(zm3, zm3, prefix16, pool_w, pool_scale.reshape(1, 1024))
    return out.reshape(bsz * t, 1024)


def _ssd_kernel(xs_ref, bm_ref, cm_ref, dt_ref, mz_ref, pre_ref, init_ref, cw_ref, cb_ref, dtb_ref, alx_ref,
                alp_ref, dsk_ref, mn_ref, ex_ref, y_ref, fin_ref, xbuf, st, *, rows_in, valid_len):
    c = pl.program_id(1)
    nc = pl.num_programs(1)
    L = SSD_CHUNK

    @pl.when(c == 0)
    def _():
        xbuf[0:8, :] = pre_ref[0]
        for pr in range(8):
            st[:, pr * 128:(pr + 1) * 128] = init_ref[0, pr * 128:(pr + 1) * 128, :].T

    if rows_in < L:
        xbuf[8 + rows_in:8 + L, :] = jnp.zeros((L - rows_in, CONV_CH), F32)
    xbuf[8:8 + rows_in, 0:1024] = xs_ref[0]
    xbuf[8:8 + rows_in, 1024:1280] = bm_ref[0]
    xbuf[8:8 + rows_in, 1280:1536] = cm_ref[0]
    acc = cb_ref[...] + xbuf[5:5 + L, :] * cw_ref[0:1, :]
    for j in range(1, CONV_W):
        acc = acc + xbuf[5 + j:5 + j + L, :] * cw_ref[j:j + 1, :]
    xbuf[0:8, :] = xbuf[L:L + 8, :]
    xc = _silu(acc)
    xs = xc[:, 0:1024]
    bmat = xc[:, 1024:1280]
    cmat = xc[:, 1280:1536]

    if rows_in < L:
        dtz = jnp.concatenate([dt_ref[0], jnp.zeros((L - rows_in, 128), F32)], axis=0)
    else:
        dtz = dt_ref[0]
    dtz = dtz + dtb_ref[...]
    dt = jnp.maximum(dtz, 0.0) + jnp.log1p(jnp.exp(-jnp.abs(dtz)))
    row = c * L + lax.broadcasted_iota(jnp.int32, (L, 1), 0)
    dt = jnp.where(row < valid_len, dt, 0.0)

    li = lax.broadcasted_iota(jnp.int32, (L, L), 0)
    si = lax.broadcasted_iota(jnp.int32, (L, L), 1)
    tril_b = li >= si
    tril = tril_b.astype(F32)
    dt_x = jnp.dot(dt, ex_ref[...], precision=HI, preferred_element_type=F32)
    la_x = dt_x * (-jnp.exp(alx_ref[...]))
    acs_x = jnp.dot(tril, la_x, precision=HI, preferred_element_type=F32)
    la_p = dt * (-jnp.exp(alp_ref[...]))
    acs_pt = jnp.dot(tril, la_p, precision=HI, preferred_element_type=F32).T

    xdt = xs * dt_x
    a_end = acs_x[L - 1:L, :]
    eacs = jnp.exp(acs_x)
    xw = (xdt * jnp.exp(a_end - acs_x)).astype(BF16)
    xdt_b = xdt.astype(BF16)
    lane = lax.broadcasted_iota(jnp.int32, (1, 128), 1)
    y_parts = []
    for g in range(M_GROUPS):
        bg = bmat[:, g * M_STATE:(g + 1) * M_STATE]
        cg = cmat[:, g * M_STATE:(g + 1) * M_STATE].astype(BF16)
        gmat = _nt(cg, bg.astype(BF16))
        bgt = bg.T.astype(BF16)
        for hp in range(M_HEADS // M_GROUPS // 2):
            pr = g * (M_HEADS // M_GROUPS // 2) + hp
            cs = slice(pr * 128, (pr + 1) * 128)
            xpair = xdt_b[:, cs]
            ydiag = jnp.zeros((L, 128), F32)
            for half in range(2):
                h = 2 * pr + half
                col = acs_x[:, h * M_HEAD_DIM:h * M_HEAD_DIM + 1]
                rw = acs_pt[h:h + 1, :]
                dec = jnp.where(tril_b, jnp.exp(col - rw), 0.0)
                mm = (gmat * dec).astype(BF16)
                in_half = (lane >= half * M_HEAD_DIM) & (lane < (half + 1) * M_HEAD_DIM)
                rhs = jnp.where(in_half, xpair, jnp.zeros_like(xpair))
                ydiag = ydiag + jnp.dot(mm, rhs, preferred_element_type=F32)
            prev = st[:, cs]
            yoff = jnp.dot(cg, prev.astype(BF16), preferred_element_type=F32) * eacs[:, cs]
            st[:, cs] = jnp.exp(a_end[:, cs]) * prev + jnp.dot(bgt, xw[:, cs], preferred_element_type=F32)
            y_parts.append(ydiag + yoff)
    y = jnp.concatenate(y_parts, axis=1)
    y = y + dsk_ref[...] * xs
    if rows_in < L:
        mz = jnp.concatenate([mz_ref[0], jnp.zeros((L - rows_in, 1024), F32)], axis=0)
    else:
        mz = mz_ref[0]
    y = y * _silu(mz)
    half_w = BRANCH_WIDTH // M_GROUPS
    outs = []
    for g in range(M_GROUPS):
        yg = y[:, g * half_w:(g + 1) * half_w]
        outs.append(yg * lax.rsqrt(jnp.mean(yg * yg, axis=-1, keepdims=True) + EPS))
    yn = jnp.concatenate(outs, axis=1) * mn_ref[...]
    y_ref[0] = yn[0:rows_in].astype(y_ref.dtype)

    @pl.when(c == nc - 1)
    def _():
        for pr in range(8):
            fin_ref[0, pr * 128:(pr + 1) * 128, :] = st[:, pr * 128:(pr + 1) * 128].T


def _ssd(z_main, z_small, prefix8, init, consts, bsz, t, rows_in, valid_len):
    conv_w8, conv_b, dtb, alx, alp, dskx, mnorm, expand = consts
    zm3 = z_main.reshape(bsz, t, N_MAIN)
    zs3 = z_small.reshape(bsz, t, N_SMALL)
    nc = t // rows_in
    full = lambda shp: pl.BlockSpec(shp, lambda b, c: tuple(0 for _ in shp))
    y, fin = pl.pallas_call(
        functools.partial(_ssd_kernel, rows_in=rows_in, valid_len=valid_len),
        grid=(bsz, nc),
        in_specs=[pl.BlockSpec((1, rows_in, 1024), lambda b, c: (b, c, C_XS // 1024)),
                  pl.BlockSpec((1, rows_in, 256), lambda b, c: (b, c, C_B // 256)),
                  pl.BlockSpec((1, rows_in, 256), lambda b, c: (b, c, C_C // 256)),
                  pl.BlockSpec((1, rows_in, 128), lambda b, c: (b, c, 0)),
                  pl.BlockSpec((1, rows_in, 1024), lambda b, c: (b, c, C_MZ // 1024)),
                  pl.BlockSpec((1, 8, CONV_CH), lambda b, c: (b, 0, 0)),
                  pl.BlockSpec((1, 1024, M_STATE), lambda b, c: (b, 0, 0)),
                  full((8, CONV_CH)), full((1, CONV_CH)), full((1, 128)), full((1, 1024)), full((1, 128)),
                  full((1, 1024)), full((1, 1024)), full((128, 1024))],
        out_specs=[pl.BlockSpec((1, rows_in, 1024), lambda b, c: (b, c, 0)),
                   pl.BlockSpec((1, 1024, M_STATE), lambda b, c: (b, 0, 0))],
        out_shape=[jax.ShapeDtypeStruct((bsz, t, 1024), BF16),
                   jax.ShapeDtypeStruct((bsz, 1024, M_STATE), F32)],
        scratch_shapes=[pltpu.VMEM((8 + SSD_CHUNK, CONV_CH), F32), pltpu.VMEM((M_STATE, 1024), F32)],
        compiler_params=_cparams(("arbitrary", "arbitrary")),
        name="ssd",
    )(zm3, zm3, zm3, zs3, zm3, prefix8, init, conv_w8, conv_b, dtb, alx, alp, dskx, mnorm, expand)
    return y.reshape(bsz * t, 1024), fin


def _overlap_matrix(n, n_cmp, n_slc, nsp):
    i = np.arange(n)[:, None]
    j = np.arange(nsp)[None, :]
    ov = (i * CMP_STRIDE < j * SLC_BLK + SLC_BLK) & (i * CMP_STRIDE + CMP_LEN > j * SLC_BLK)
    ov = ov & (i < n_cmp) & (j < n_slc)
    return jnp.asarray(ov.astype(np.float32), dtype=BF16)


def _rope_tables(pos):
    inv = ROPE_THETA ** (-jnp.arange(0, HEAD_DIM, 2, dtype=F32) / HEAD_DIM)
    ang = pos.astype(F32)[:, None] * inv[None, :]
    cos, sin = jnp.cos(ang), jnp.sin(ang)
    return jnp.concatenate([cos, cos], axis=1), jnp.concatenate([-sin, sin], axis=1)


def _layer_consts(l, w_in, qk_gain, cmp_pe, cmp_w, pool_w, conv_w, conv_b, dt_bias, a_log, d_skip, mnorm_w,
                  w_branch, w_out):
    offs = np.cumsum((0, 1024, 1536, 24, 1024, 1024, 1024, 1024, CONV_CH, M_HEADS, 3 * D_MODEL))
    seg = lambda k: w_in[l][:, offs[k]:offs[k + 1]]
    xbc = seg(7)
    w_main = jnp.concatenate([seg(0), seg(3), seg(4), seg(5), seg(6), xbc[:, :1024], seg(9), seg(1),
                              xbc[:, 1024:1280], xbc[:, 1280:1536]], axis=1).astype(BF16)
    gates = seg(2)
    zpad = lambda w: jnp.zeros((D_MODEL, w), F32)
    w_small = jnp.concatenate([seg(8), zpad(128 - M_HEADS), gates[:, :12], zpad(116), gates[:, 12:], zpad(116)],
                              axis=1).astype(BF16)
    gain8 = jnp.concatenate([qk_gain[l], jnp.zeros((4, HEAD_DIM), F32)], axis=0)
    w4 = cmp_w[l].reshape(2, 2, 8, 2, HEAD_DIM, HEAD_DIM)
    wcat = jnp.transpose(w4, (0, 2, 3, 4, 1, 5)).reshape(2, 8, 256, 256).astype(BF16)
    pe4 = cmp_pe[l].reshape(2, 2, 8, 2 * HEAD_DIM)
    pe_lhs = jnp.concatenate([jnp.transpose(pe4, (0, 2, 1, 3)), jnp.zeros((2, 8, 14, 256), F32)], axis=2)
    rep = lambda v: jnp.repeat(v, M_HEAD_DIM).reshape(1, 1024)
    pad128 = lambda v: jnp.concatenate([v, jnp.zeros((128 - v.shape[0],), F32)]).reshape(1, 128)
    expand = (np.arange(128)[:, None] == (np.arange(1024)[None, :] // M_HEAD_DIM)).astype(np.float32)
    ssd_consts = (jnp.concatenate([conv_w[l], jnp.zeros((4, CONV_CH), F32)], axis=0), conv_b[l].reshape(1, CONV_CH),
                  pad128(dt_bias[l]), rep(a_log[l]), pad128(a_log[l]), rep(d_skip[l]), mnorm_w[l].reshape(1, 1024),
                  jnp.asarray(expand))
    return dict(w_main=w_main, w_small=w_small, gain8=gain8, wcat=wcat, pe_lhs=pe_lhs,
                pool_w=pool_w[l].astype(BF16), ssd=ssd_consts, w_branch=w_branch[l].astype(BF16),
                w_out=w_out[l].astype(BF16))


def _xbc_cols(z3):
    return jnp.concatenate([z3[..., C_XS:C_XS + 1024], z3[..., C_B:C_B + 256], z3[..., C_C:C_C + 256]], axis=-1)


def _tile(m, pref):
    for t in pref:
        if m % t == 0:
            return t
    return m


def kernel(x_prompt, x_sample, cache_kv, cache_win, state_pool, state_conv, state_ssm, page_table, norm_w, w_in,
           qk_gain, cmp_pe, cmp_w, pool_w, pool_scale, conv_w, conv_b, dt_bias, a_log, d_skip, mnorm_w, w_branch,
           w_out):
    depth = w_in.shape[0]
    bsz, seq, _ = x_prompt.shape
    dec_b, dec_t, _ = x_sample.shape
    n_pool, page = cache_kv.shape[1], cache_kv.shape[2]
    n_pages = page_table.shape[1]
    past_len = n_pages * page
    assert page == 128 and dec_t == 4 and cache_win.shape[2] == WINDOW and seq % 128 == 0
    nq = 8
    cache_pages = cache_kv.reshape(depth, n_pool, page * 4 * N_KV_HEADS, HEAD_DIM)
    cache_win_rows = cache_win.reshape(depth, dec_b, WINDOW * 2 * N_KV_HEADS, HEAD_DIM)
    tk_p = _tile(seq, (512, 256, 128))

    mp = bsz * seq
    ms = dec_b * nq
    xp = x_prompt.reshape(mp, D_MODEL)
    xs = jnp.pad(x_sample, ((0, 0), (0, nq - dec_t), (0, 0))).reshape(ms, D_MODEL)

    cos_p, sin_p = _rope_tables(jnp.tile(jnp.arange(seq), bsz))
    cos_s, sin_s = _rope_tables(jnp.tile(past_len + jnp.arange(nq), dec_b))

    n_p = seq // CMP_STRIDE
    n_slc_p = -(-seq // SLC_BLK)
    nsb_p = -(-n_slc_p // 16) * 16
    assert nsb_p <= 128
    ovt_p = _overlap_matrix(n_p, n_p - 1, n_slc_p, nsb_p).T
    n_s = n_pages * 8
    tk = past_len + dec_t
    n_slc_s = -(-tk // SLC_BLK)
    nsp_s = -(-n_slc_s // 128) * 128
    ov_s = _overlap_matrix(n_s, tk // CMP_STRIDE - 1, n_slc_s, nsp_s)

    tm_p = _tile(mp, (512, 256, 128))
    outs_p = [[] for _ in range(3)]
    outs_s = [[] for _ in range(3)]
    rows_p = jnp.zeros((depth, mp * 8, HEAD_DIM), F32)
    win_p = jnp.zeros((depth, mp * 4, HEAD_DIM), F32)
    rows_s = jnp.zeros((depth, ms * 8, HEAD_DIM), F32)
    win_s = jnp.zeros((depth, ms * 4, HEAD_DIM), F32)
    hp = _rmsnorm(xp, norm_w[0], tm_p)
    hs = _rmsnorm(xs, norm_w[0], ms)
    for l in range(depth):
        c = _layer_consts(l, w_in, qk_gain, cmp_pe, cmp_w, pool_w, conv_w, conv_b, dt_bias, a_log, d_skip,
                          mnorm_w, w_branch, w_out)
        g_next = norm_w[min(l + 1, depth - 1)]
        z_main = _matmul(hp, c["w_main"], tm_p, 1024, "inproj")
        z_small = _matmul(hp, c["w_small"], tm_p, N_SMALL, "inproj_small")
        q_r, rows_p, win_p, cmpin, kv_att, v_t = _prep(z_main, cos_p, sin_p, c["gain8"], _tile(mp, (256, 128)),
                                                       True, l, depth, rows_p, win_p)
        r_cmp, c_cmp = _compress_prompt(cmpin, c["wcat"], c["pe_lhs"], bsz, seq)
        nsa_out = _nsa_prompt(q_r, r_cmp, c_cmp, kv_att, v_t, ovt_p, z_small, z_main, bsz, seq, tk_p)
        pool_out = _pool(z_main, jnp.zeros((bsz, 16, 1024), F32), c["pool_w"], pool_scale[l], bsz, seq,
                         _tile(seq, (256, 128)), 0)
        m_out, fin = _ssd(z_main, z_small, jnp.zeros((bsz, 8, CONV_CH), F32),
                          jnp.zeros((bsz, 1024, M_STATE), F32), c["ssd"], bsz, seq, SSD_CHUNK, seq)
        merged = _merge(nsa_out, pool_out, m_out, c["w_branch"], z_main, tm_p, 1024)
        xp, hp = _outproj(xp, merged, c["w_out"], g_next, tm_p)
        zm3 = z_main.reshape(bsz, seq, N_MAIN)
        outs_p[0].append(zm3[:, seq - POOL_STATE:, C_POOLU:C_POOLU + 1024])
        outs_p[1].append(_xbc_cols(zm3[:, seq - (CONV_W - 1):]))
        outs_p[2].append(fin.reshape(bsz, M_HEADS, M_HEAD_DIM, M_STATE))
        z_main = _matmul(hs, c["w_main"], ms, 1024, "inproj_s")
        z_small = _matmul(hs, c["w_small"], ms, N_SMALL, "inproj_small_s")
        q_r, rows_s, win_s, _, kv_att = _prep(z_main, cos_s, sin_s, c["gain8"], ms, False, l, depth, rows_s, win_s)
        new_att = jnp.pad(kv_att.reshape(8, dec_b, nq, HEAD_DIM), ((0, 0), (0, 0), (0, 128 - nq), (0, 0)))
        new_att = jnp.where((jnp.arange(128) < dec_t)[None, None, :, None], new_att, jnp.zeros_like(new_att))
        nsa_out = _nsa_decode(page_table, q_r, cache_pages, c["wcat"], c["pe_lhs"], new_att, cache_win_rows, ov_s,
                              z_small, z_main, l, past_len, nq, dec_t, _tile(n_pages, (16, 8, 4, 2)))
        pre16 = jnp.pad(state_pool[l], ((0, 0), (16 - POOL_STATE, 0), (0, 0)))
        pool_out = _pool(z_main, pre16, c["pool_w"], pool_scale[l], dec_b, nq, nq, past_len)
        pre8 = jnp.pad(state_conv[l], ((0, 0), (8 - (CONV_W - 1), 0), (0, 0)))
        m_out, fin = _ssd(z_main, z_small, pre8, state_ssm[l].reshape(dec_b, 1024, M_STATE), c["ssd"], dec_b, nq,
                          nq, dec_t)
        merged = _merge(nsa_out, pool_out, m_out, c["w_branch"], z_main, ms, 1024)
        xs, hs = _outproj(xs, merged, c["w_out"], g_next, ms)
        zm3 = z_main.reshape(dec_b, nq, N_MAIN)
        outs_s[0].append(jnp.concatenate([state_pool[l], zm3[:, :dec_t, C_POOLU:C_POOLU + 1024]],
                                         axis=1)[:, -POOL_STATE:])
        outs_s[1].append(jnp.concatenate([state_conv[l], _xbc_cols(zm3[:, :dec_t])], axis=1)[:, -(CONV_W - 1):])
        outs_s[2].append(fin.reshape(dec_b, M_HEADS, M_HEAD_DIM, M_STATE))

    y_p = xp.reshape(bsz, seq, D_MODEL)
    y_s = xs.reshape(dec_b, nq, D_MODEL)[:, :dec_t]
    wlen = min(WINDOW, seq)
    kv_p = rows_p.reshape(depth, bsz, seq, 4, N_KV_HEADS, HEAD_DIM)
    win_p = win_p.reshape(depth, bsz, seq, 2, N_KV_HEADS, HEAD_DIM)[:, :, seq - wlen:]
    kv_s = rows_s.reshape(depth, dec_b, nq, 4, N_KV_HEADS, HEAD_DIM)[:, :, :dec_t]
    win_new = win_s.reshape(depth, dec_b, nq, 2, N_KV_HEADS, HEAD_DIM)[:, :, :dec_t]
    win_s = jnp.concatenate([cache_win, win_new], axis=2)[:, :, -WINDOW:]
    stk = lambda lst: jnp.stack(lst)
    return (y_p, y_s, kv_p, win_p, *[stk(v) for v in outs_p], kv_s, win_s, *[stk(v) for v in outs_s])
```

```python
import functools

import numpy as np
import jax
import jax.numpy as jnp
from jax import lax
from jax.experimental import pallas as pl
from jax.experimental.pallas import tpu as pltpu

F32 = jnp.float32
BF16 = jnp.bfloat16

D_MODEL = 2048
BRANCH_WIDTH = D_MODEL // 2
HEAD_DIM = 128
N_HEADS = BRANCH_WIDTH // HEAD_DIM
GQA_REP = 4
N_KV_HEADS = N_HEADS // GQA_REP
ROPE_THETA = 10000.0
CMP_LEN = 32
CMP_STRIDE = 16
SLC_BLK = 64
SLC_SHIFT = SLC_BLK.bit_length() - 1
N_SEL = 16
WINDOW = 512
Q_BLOCK = 128
SEL_BONUS = 1.0e4
POOL_WINDOWS = (2, 4, 8, 16)
POOL_GROUP = BRANCH_WIDTH // len(POOL_WINDOWS)
POOL_STATE = max(POOL_WINDOWS) - 1
M_HEAD_DIM = 64
M_HEADS = BRANCH_WIDTH // M_HEAD_DIM
M_STATE = 128
M_GROUPS = 2
CONV_W = 4
CONV_CH = BRANCH_WIDTH + 2 * M_GROUPS * M_STATE
SSD_CHUNK = 128
EPS = 1e-6
SCALE = HEAD_DIM ** -0.5
NEG_BIG = -1.0e30

C_Q = 0
C_NSAZ = 1024
C_POOLU = 2048
C_POOLZ = 3072
C_MZ = 4096
C_XS = 5120
C_MG = 6144
C_KV = 12288
C_B = 13824
C_C = 14080
N_MAIN = 14336
N_SMALL = 384

VMEM_LIMIT = 56 * 1024 * 1024
HI = lax.Precision.HIGHEST


def _cparams(sem):
    return pltpu.CompilerParams(dimension_semantics=sem, vmem_limit_bytes=VMEM_LIMIT)


def _nt(a, b):
    return lax.dot_general(a, b, (((1,), (1,)), ((), ())), preferred_element_type=F32)


def _silu(x):
    return x * jax.nn.sigmoid(x)


def _rmsnorm_kernel(x_ref, g_ref, o_ref):
    x = x_ref[...]
    ms = jnp.mean(x * x, axis=-1, keepdims=True)
    o_ref[...] = (x * lax.rsqrt(ms + EPS) * g_ref[...]).astype(o_ref.dtype)


def _rmsnorm(x, g, tr):
    m, d = x.shape
    return pl.pallas_call(
        _rmsnorm_kernel,
        grid=(m // tr,),
        in_specs=[pl.BlockSpec((tr, d), lambda i: (i, 0)), pl.BlockSpec((1, d), lambda i: (0, 0))],
        out_specs=pl.BlockSpec((tr, d), lambda i: (i, 0)),
        out_shape=jax.ShapeDtypeStruct((m, d), BF16),
        compiler_params=_cparams(("parallel",)),
        name="rmsnorm",
    )(x, g.reshape(1, d))


def _mm_kernel(a_ref, b_ref, o_ref):
    o_ref[...] = jnp.dot(a_ref[...], b_ref[...], preferred_element_type=F32).astype(o_ref.dtype)


def _matmul(a, b, tm, tn, name, out_dtype=F32):
    m, k = a.shape
    n = b.shape[1]
    return pl.pallas_call(
        _mm_kernel,
        grid=(m // tm, n // tn),
        in_specs=[pl.BlockSpec((tm, k), lambda i, j: (i, 0)), pl.BlockSpec((k, tn), lambda i, j: (0, j))],
        out_specs=pl.BlockSpec((tm, tn), lambda i, j: (i, j)),
        out_shape=jax.ShapeDtypeStruct((m, n), out_dtype),
        compiler_params=_cparams(("parallel", "parallel")),
        name=name,
    )(a, b)


def _merge_kernel(a_ref, p_ref, m_ref, w_ref, g0_ref, g1_ref, g2_ref, o_ref):
    gate = lambda g_ref: jax.nn.sigmoid(g_ref[...].astype(F32))
    acc = gate(g0_ref) * jnp.dot(a_ref[...], w_ref[0], preferred_element_type=F32)
    acc = acc + gate(g1_ref) * jnp.dot(p_ref[...], w_ref[1], preferred_element_type=F32)
    acc = acc + gate(g2_ref) * jnp.dot(m_ref[...], w_ref[2], preferred_element_type=F32)
    o_ref[...] = acc.astype(o_ref.dtype)


def _merge(nsa_out, pool_out, m_out, w_branch, z_main, tm, tn):
    m = nsa_out.shape[0]
    gblk = [(C_MG + b * D_MODEL) // tn for b in range(3)]
    act = pl.BlockSpec((tm, BRANCH_WIDTH), lambda i, j: (i, 0))
    return pl.pallas_call(
        _merge_kernel,
        grid=(m // tm, D_MODEL // tn),
        in_specs=[act, act, act,
                  pl.BlockSpec((3, BRANCH_WIDTH, tn), lambda i, j: (0, 0, j)),
                  pl.BlockSpec((tm, tn), lambda i, j: (i, gblk[0] + j)),
                  pl.BlockSpec((tm, tn), lambda i, j: (i, gblk[1] + j)),
                  pl.BlockSpec((tm, tn), lambda i, j: (i, gblk[2] + j))],
        out_specs=pl.BlockSpec((tm, tn), lambda i, j: (i, j)),
        out_shape=jax.ShapeDtypeStruct((m, D_MODEL), BF16),
        compiler_params=_cparams(("parallel", "parallel")),
        name="merge",
    )(nsa_out, pool_out, m_out, w_branch, z_main, z_main, z_main)


def _outproj_kernel(x_ref, a_ref, w_ref, g_ref, o_ref, h_ref):
    y = x_ref[...] + jnp.dot(a_ref[...], w_ref[...], preferred_element_type=F32)
    o_ref[...] = y
    ms = jnp.mean(y * y, axis=-1, keepdims=True)
    h_ref[...] = (y * lax.rsqrt(ms + EPS) * g_ref[...]).astype(h_ref.dtype)


def _outproj(x, merged, w_out, g_next, tm):
    m = x.shape[0]
    row = pl.BlockSpec((tm, D_MODEL), lambda i: (i, 0))
    return pl.pallas_call(
        _outproj_kernel,
        grid=(m // tm,),
        in_specs=[row, row, pl.BlockSpec((D_MODEL, D_MODEL), lambda i: (0, 0)),
                  pl.BlockSpec((1, D_MODEL), lambda i: (0, 0))],
        out_specs=[row, row],
        out_shape=[jax.ShapeDtypeStruct((m, D_MODEL), F32), jax.ShapeDtypeStruct((m, D_MODEL), BF16)],
        compiler_params=_cparams(("parallel",)),
        name="outproj",
    )(x, merged, w_out, g_next.reshape(1, D_MODEL))


def _prep_kernel(zq_ref, zkv_ref, cos_ref, sin_ref, gain_ref, rows_in_ref, win_in_ref,
                 q_ref, rows_ref, win_ref, cmpin_ref, kvatt_ref, *maybe_vt_ref, tr):
    del rows_in_ref, win_in_ref
    cos = cos_ref[...]
    sin = sin_ref[...]

    def norm_rope(x, gi):
        ms = jnp.mean(x * x, axis=-1, keepdims=True)
        y = x * lax.rsqrt(ms + EPS) * gain_ref[gi:gi + 1, :]
        return y * cos + pltpu.roll(y, HEAD_DIM // 2, 1) * sin

    for h in range(N_HEADS):
        sl = slice(h * HEAD_DIM, (h + 1) * HEAD_DIM)
        q_ref[:, sl] = (norm_rope(zq_ref[:, sl].astype(F32), 0) * SCALE).astype(q_ref.dtype)
    for br in range(3):
        for g in range(N_KV_HEADS):
            ck = br * 512 + g * HEAD_DIM
            cv = br * 512 + 256 + g * HEAD_DIM
            k = norm_rope(zkv_ref[:, ck:ck + HEAD_DIM].astype(F32), 1 + br)
            v = zkv_ref[:, cv:cv + HEAD_DIM].astype(F32)
            if br < 2:
                rows_ref[0, pl.ds(br * 4 + g, tr, stride=8), :] = k
                rows_ref[0, pl.ds(br * 4 + 2 + g, tr, stride=8), :] = v
            else:
                win_ref[0, pl.ds(g, tr, stride=4), :] = k
                win_ref[0, pl.ds(2 + g, tr, stride=4), :] = v
            if br == 0:
                cmpin_ref[g] = k
                cmpin_ref[2 + g] = v
            else:
                base = (br - 1) * 4
                kvatt_ref[base + g] = k.astype(kvatt_ref.dtype)
                kvatt_ref[base + 2 + g] = v.astype(kvatt_ref.dtype)
                if maybe_vt_ref:
                    for j in range(tr // 128):
                        maybe_vt_ref[0][(br - 1) * 2 + g, j] = v[j * 128:(j + 1) * 128].T.astype(kvatt_ref.dtype)


def _prep(z_main, cos2, sin2, gain8, tr, emit_vt, layer, depth, rows_all, win_all):
    m = z_main.shape[0]
    vt_specs = [pl.BlockSpec((4, tr // 128, HEAD_DIM, 128), lambda i: (0, i, 0, 0))] if emit_vt else []
    vt_shapes = [jax.ShapeDtypeStruct((4, m // 128, HEAD_DIM, 128), BF16)] if emit_vt else []
    return pl.pallas_call(
        functools.partial(_prep_kernel, tr=tr),
        grid=(m // tr,),
        in_specs=[pl.BlockSpec((tr, 1024), lambda i: (i, C_Q // 1024)),
                  pl.BlockSpec((tr, 1536), lambda i: (i, C_KV // 1536)),
                  pl.BlockSpec((tr, HEAD_DIM), lambda i: (i, 0)),
                  pl.BlockSpec((tr, HEAD_DIM), lambda i: (i, 0)),
                  pl.BlockSpec((8, HEAD_DIM), lambda i: (0, 0)),
                  pl.BlockSpec(memory_space=pl.ANY), pl.BlockSpec(memory_space=pl.ANY)],
        out_specs=[pl.BlockSpec((tr, 1024), lambda i: (i, 0)),
                   pl.BlockSpec((1, tr * 8, HEAD_DIM), lambda i: (layer, i, 0)),
                   pl.BlockSpec((1, tr * 4, HEAD_DIM), lambda i: (layer, i, 0)),
                   pl.BlockSpec((4, tr, HEAD_DIM), lambda i: (0, i, 0)),
                   pl.BlockSpec((8, tr, HEAD_DIM), lambda i: (0, i, 0))] + vt_specs,
        out_shape=[jax.ShapeDtypeStruct((m, 1024), BF16),
                   jax.ShapeDtypeStruct((depth, m * 8, HEAD_DIM), F32),
                   jax.ShapeDtypeStruct((depth, m * 4, HEAD_DIM), F32),
                   jax.ShapeDtypeStruct((4, m, HEAD_DIM), F32),
                   jax.ShapeDtypeStruct((8, m, HEAD_DIM), BF16)] + vt_shapes,
        input_output_aliases={5: 1, 6: 2},
        compiler_params=_cparams(("parallel",)),
        name="nsa_prep",
    )(z_main, z_main, cos2, sin2, gain8, rows_all, win_all)


def _compress_rows(load_pair, w_ref, c, n):
    acc = jnp.zeros((n, 256), F32)
    for j in range(8):
        xa, xb = load_pair(j)
        lhs = jnp.concatenate([xa, xb], axis=1).astype(BF16)
        acc = acc + jnp.dot(lhs, w_ref[c, j], preferred_element_type=F32)
    return acc


def _compress_const(pe_ref, w_ref, c):
    acc = jnp.zeros((16, 256), F32)
    for j in range(8):
        acc = acc + jnp.dot(pe_ref[c, j].astype(BF16), w_ref[c, j], preferred_element_type=F32)
    return acc[0:1, 0:128] + acc[1:2, 128:256]


def _compress_prompt_kernel(x_ref, w_ref, pe_ref, r_ref, c_ref, *, n):
    c = pl.program_id(1) // 2

    def load_pair(j):
        return (x_ref[0, 0, pl.ds(2 * j, n, stride=16), :],
                x_ref[0, 0, pl.ds(2 * j + 1, n, stride=16), :])

    r_ref[0, 0, 0:n, :] = _compress_rows(load_pair, w_ref, c, n)
    r_ref[0, 0, n:n + 8, :] = jnp.zeros((8, 256), F32)
    c_ref[0, 0] = jnp.broadcast_to(_compress_const(pe_ref, w_ref, c), (8, 128))


def _compress_prompt(cmpin, wcat, pe_lhs, bsz, t):
    n = t // CMP_STRIDE
    x = cmpin.reshape(4, bsz, t, HEAD_DIM)
    return pl.pallas_call(
        functools.partial(_compress_prompt_kernel, n=n),
        grid=(bsz, 4),
        in_specs=[pl.BlockSpec((1, 1, t, HEAD_DIM), lambda b, cg: (cg, b, 0, 0)),
                  pl.BlockSpec((2, 8, 256, 256), lambda b, cg: (0, 0, 0, 0)),
                  pl.BlockSpec((2, 8, 16, 256), lambda b, cg: (0, 0, 0, 0))],
        out_specs=[pl.BlockSpec((1, 1, n + 8, 256), lambda b, cg: (b, cg, 0, 0)),
                   pl.BlockSpec((1, 1, 8, 128), lambda b, cg: (b, cg, 0, 0))],
        out_shape=[jax.ShapeDtypeStruct((bsz, 4, n + 8, 256), F32),
                   jax.ShapeDtypeStruct((bsz, 4, 8, 128), F32)],
        compiler_params=_cparams(("parallel", "parallel")),
        name="compress_prompt",
    )(x, wcat, pe_lhs)


def _combine_cmp(r, cc, n, n_cmp):
    a = r[0:n, 0:128]
    bsh = r[1:n + 1, 128:256]
    ci = lax.broadcasted_iota(jnp.int32, (n, 1), 0)
    return jnp.where(ci < n_cmp, a + bsh + cc, 0.0)


def _softmax_rows(s, mask):
    s = jnp.where(mask, s, -jnp.inf)
    m = jnp.max(s, axis=-1, keepdims=True)
    m = jnp.where(m > -jnp.inf, m, 0.0)
    e = jnp.exp(s - m)
    return e / jnp.maximum(jnp.sum(e, axis=-1, keepdims=True), 1e-30)


def _cmp_and_select(q4, kc, vc, ov, t_q, n, n_cmp, n_slc, n_sel, nsp):
    nq = t_q.shape[0]
    t_rows = jnp.concatenate([t_q] * GQA_REP, axis=0)
    sc = _nt(q4, kc)
    ci = lax.broadcasted_iota(jnp.int32, (1, n), 1)
    cmask = (ci * CMP_STRIDE + (CMP_LEN - 1) <= t_rows) & (ci < n_cmp)
    pc = _softmax_rows(sc, cmask)
    o_cmp = jnp.dot(pc.astype(BF16), vc, preferred_element_type=F32)
    pcs = pc[0:nq] + pc[nq:2 * nq] + pc[2 * nq:3 * nq] + pc[3 * nq:4 * nq]
    hi = pcs.astype(BF16)
    lo = (pcs - hi.astype(F32)).astype(BF16)
    imp = jnp.dot(hi, ov, preferred_element_type=F32) + jnp.dot(lo, ov, preferred_element_type=F32)
    j = lax.broadcasted_iota(jnp.int32, (1, nsp), 1)
    cur = t_q >> SLC_SHIFT
    forced = (j == 0) | (j == cur) | (j == cur - 1)
    valid = (j * SLC_BLK <= t_q) & (j < n_slc)
    score = jnp.where(valid, imp + SEL_BONUS * forced.astype(F32), -jnp.inf)
    rank = jnp.zeros((nq, nsp), F32)
    for k in range(n_slc):
        col = score[:, k:k + 1]
        before = (col > score) | ((col == score) & (k < j))
        rank = rank + before.astype(F32)
    sel = (rank < float(n_sel)) & (j < n_slc)
    return o_cmp, sel.astype(BF16)


def _sel_mask(sel_win, win0, blk0, t_q, kpos0, nk):
    jb = win0 + lax.broadcasted_iota(jnp.int32, (128, nk), 0)
    kk = lax.broadcasted_iota(jnp.int32, (128, nk), 1)
    expand = (jb == blk0 + (kk >> SLC_SHIFT)).astype(BF16)
    hit = jnp.dot(sel_win, expand, preferred_element_type=F32)
    kpos = kpos0 + lax.broadcasted_iota(jnp.int32, (1, nk), 1)
    return (hit > 0.5) & (kpos <= t_q)


def _gate_and_write(o_ref_write, gts, nz, o_cmp, o_slc, o_win, nq):
    for r in range(GQA_REP):
        rs = slice(r * nq, (r + 1) * nq)
        o = (gts[:, 3 * r:3 * r + 1] * o_cmp[rs] + gts[:, 3 * r + 1:3 * r + 2] * o_slc[rs]
             + gts[:, 3 * r + 2:3 * r + 3] * o_win[rs])
        o_ref_write(r, o * _silu(nz[:, r * HEAD_DIM:(r + 1) * HEAD_DIM]))


def _nsa_prompt_kernel(q_ref, rk_ref, rv_ref, ck_ref, cv_ref, ks_ref, vst_ref, kw_ref, vwt_ref, ovt_ref,
                       gz_ref, nz_ref, o_ref, *, n, n_cmp, n_slc, n_sel, nsb, tk, nwt):
    i = pl.program_id(2)
    nq = Q_BLOCK
    nr = GQA_REP * nq
    qb = q_ref[0]
    q4 = jnp.concatenate([qb[:, r * HEAD_DIM:(r + 1) * HEAD_DIM] for r in range(GQA_REP)], axis=0)
    t_q = i * nq + lax.broadcasted_iota(jnp.int32, (1, nq), 1)
    t_q4 = jnp.concatenate([t_q] * GQA_REP, axis=1)

    kc = _combine_cmp(rk_ref.at[0, 0], ck_ref[0, 0, 0:1, :], n, n_cmp).astype(BF16)
    vct = _combine_cmp(rv_ref.at[0, 0], cv_ref[0, 0, 0:1, :], n, n_cmp).T.astype(BF16)
    sc = _nt(kc, q4)
    ci = lax.broadcasted_iota(jnp.int32, (n, 1), 0)
    cmask = (ci * CMP_STRIDE + (CMP_LEN - 1) <= t_q4) & (ci < n_cmp)
    sc = jnp.where(cmask, sc, -jnp.inf)
    mx = jnp.max(sc, axis=0, keepdims=True)
    mx = jnp.where(mx > -jnp.inf, mx, 0.0)
    e = jnp.exp(sc - mx)
    pc = e * (1.0 / jnp.maximum(jnp.sum(e, axis=0, keepdims=True), 1e-30))
    o_cmp = jnp.dot(vct, pc.astype(BF16), preferred_element_type=F32)
    pcs = pc[:, 0:nq] + pc[:, nq:2 * nq] + pc[:, 2 * nq:3 * nq] + pc[:, 3 * nq:4 * nq]
    hi = pcs.astype(BF16)
    lo = (pcs - hi.astype(F32)).astype(BF16)
    ovt = ovt_ref[...]
    imp = jnp.dot(ovt, hi, preferred_element_type=F32) + jnp.dot(ovt, lo, preferred_element_type=F32)

    jb = lax.broadcasted_iota(jnp.int32, (nsb, 1), 0)
    cur = t_q >> SLC_SHIFT
    forced = (jb == 0) | (jb == cur) | (jb == cur - 1)
    valid = (jb * SLC_BLK <= t_q) & (jb < n_slc)
    score = jnp.where(valid, imp + SEL_BONUS * forced.astype(F32), -jnp.inf)
    rank = jnp.zeros((nsb, nq), F32)
    for k in range(n_slc):
        row = score[k:k + 1, :]
        before = (row > score) | ((row == score) & (k < jb))
        rank = rank + before.astype(F32)
    sel = ((rank < float(n_sel)) & (jb < n_slc)).astype(BF16)
    if nsb < 128:
        sel = jnp.concatenate([sel, jnp.zeros((128 - nsb, nq), BF16)], axis=0)

    def masked_scores(k, mq):
        st = _nt(k, q4)
        return jnp.concatenate([jnp.where(mq, st[:, r * nq:(r + 1) * nq], -jnp.inf)
                                for r in range(GQA_REP)], axis=1)

    def keys_at(k_ref, vt_ref, tile0, ntiles):
        k = k_ref[0, 0, pl.ds(pl.multiple_of(tile0 * 128, 128), ntiles * 128), :]
        vt = jnp.concatenate([vt_ref[0, 0, tile0 + j] for j in range(ntiles)], axis=1)
        kpos = tile0 * 128 + lax.broadcasted_iota(jnp.int32, (ntiles * 128, 1), 0)
        return k, vt, kpos

    tpt = tk // 128

    def slc_body(kb, carry):
        m, l, acc = carry
        k, vt, kpos = keys_at(ks_ref, vst_ref, kb * tpt, tpt)
        jl = lax.broadcasted_iota(jnp.int32, (tk, 128), 1)
        expand = (jl == (kpos >> SLC_SHIFT)).astype(BF16)
        hit = jnp.dot(expand, sel, preferred_element_type=F32)
        st = masked_scores(k, (hit > 0.5) & (kpos <= t_q))
        m_new = jnp.maximum(m, jnp.max(st, axis=0, keepdims=True))
        p = jnp.exp(st - m_new)
        alpha = jnp.exp(m - m_new)
        l_new = alpha * l + jnp.sum(p, axis=0, keepdims=True)
        acc_new = alpha * acc + jnp.dot(vt, p.astype(BF16), preferred_element_type=F32)
        return m_new, l_new, acc_new

    init = (jnp.full((1, nr), NEG_BIG, F32), jnp.zeros((1, nr), F32), jnp.zeros((HEAD_DIM, nr), F32))
    m, l, acc = lax.fori_loop(0, (i * nq + nq - 1) // tk + 1, slc_body, init)
    o_slc = acc * (1.0 / jnp.maximum(l, 1e-30))

    k, vt, kpos = keys_at(kw_ref, vwt_ref, jnp.maximum(i + 1 - nwt, 0), nwt)
    d = t_q - kpos
    st = masked_scores(k, (d >= 0) & (d < WINDOW))
    mw = jnp.max(st, axis=0, keepdims=True)
    mw = jnp.where(mw > -jnp.inf, mw, 0.0)
    p = jnp.exp(st - mw)
    o_win = (jnp.dot(vt, p.astype(BF16), preferred_element_type=F32)
             * (1.0 / jnp.maximum(jnp.sum(p, axis=0, keepdims=True), 1e-30)))

    gt = jax.nn.sigmoid(gz_ref[0]).T
    nz = nz_ref[0].astype(F32)
    for r in range(GQA_REP):
        cs = slice(r * nq, (r + 1) * nq)
        ot = (gt[3 * r:3 * r + 1, :] * o_cmp[:, cs] + gt[3 * r + 1:3 * r + 2, :] * o_slc[:, cs]
              + gt[3 * r + 2:3 * r + 3, :] * o_win[:, cs])
        hs = slice(r * HEAD_DIM, (r + 1) * HEAD_DIM)
        o_ref[0, :, hs] = (ot.T * _silu(nz[:, hs])).astype(o_ref.dtype)


def _nsa_prompt(q_r, r_cmp, c_cmp, kv_att, v_t, ovt, z_small, z_main, bsz, t, tk):
    n = t // CMP_STRIDE
    n_cmp = n - CMP_LEN // CMP_STRIDE + 1
    n_slc = -(-t // SLC_BLK)
    nsb = ovt.shape[0]
    q3 = q_r.reshape(bsz, t, 1024)
    kv4 = kv_att.reshape(8, bsz, t, HEAD_DIM)
    vt5 = v_t.reshape(4, bsz, t // 128, HEAD_DIM, 128)
    nwt = min(WINDOW // Q_BLOCK + 1, t // 128)
    zs3 = z_small.reshape(bsz, t, N_SMALL)
    zm3 = z_main.reshape(bsz, t, N_MAIN)
    rspec = lambda off: pl.BlockSpec((1, 1, n + 8, 256), lambda b, g, i: (b, off + g, 0, 0))
    cspec = lambda off: pl.BlockSpec((1, 1, 8, 128), lambda b, g, i: (b, off + g, 0, 0))
    kspec = lambda off: pl.BlockSpec((1, 1, t, HEAD_DIM), lambda b, g, i: (off + g, b, 0, 0))
    vtspec = lambda off: pl.BlockSpec((1, 1, t // 128, HEAD_DIM, 128), lambda b, g, i: (off + g, b, 0, 0, 0))
    out = pl.pallas_call(
        functools.partial(_nsa_prompt_kernel, n=n, n_cmp=n_cmp, n_slc=n_slc, n_sel=min(N_SEL, n_slc), nsb=nsb,
                          tk=tk, nwt=nwt),
        grid=(bsz, N_KV_HEADS, t // Q_BLOCK),
        in_specs=[pl.BlockSpec((1, Q_BLOCK, 512), lambda b, g, i: (b, i, g)),
                  rspec(0), rspec(2), cspec(0), cspec(2),
                  kspec(0), vtspec(0), kspec(4), vtspec(2),
                  pl.BlockSpec(ovt.shape, lambda b, g, i: (0, 0)),
                  pl.BlockSpec((1, Q_BLOCK, 128), lambda b, g, i: (b, i, 1 + g)),
                  pl.BlockSpec((1, Q_BLOCK, 512), lambda b, g, i: (b, i, C_NSAZ // 512 + g))],
        out_specs=pl.BlockSpec((1, Q_BLOCK, 512), lambda b, g, i: (b, i, g)),
        out_shape=jax.ShapeDtypeStruct((bsz, t, 1024), BF16),
        compiler_params=_cparams(("parallel", "parallel", "arbitrary")),
        name="nsa_prompt",
    )(q3, r_cmp, r_cmp, c_cmp, c_cmp, kv4, vt5, kv4, vt5, ovt, zs3, zm3)
    return out.reshape(bsz * t, 1024)


def _nsa_decode_kernel(pt_ref, cache_ref, q_ref, w_ref, pe_ref, new_ref, cw_ref, ov_ref, zs_ref, nz_ref, o_ref,
                       buf, sem, r_scr, sc_scr, v_scr,
                       *, layer, pp, ns, n, n_cmp, n_slc, n_sel, nsp, past_len, nq):
    b = pl.program_id(0)
    s = pl.program_id(1)
    step = b * ns + s
    nsteps = pl.num_programs(0) * ns
    slot = step % 2
    nch = pp * 8
    nk = pp * 128
    nr = GQA_REP * nq

    def copies(bb, ss, sl):
        return [pltpu.make_async_copy(cache_ref.at[layer, pt_ref[bb, ss * pp + j]],
                                      buf.at[sl, pl.ds(j * 1024, 1024), :], sem.at[sl]) for j in range(pp)]

    @pl.when(step == 0)
    def _():
        for cp in copies(b, s, slot):
            cp.start()

    @pl.when(step + 1 < nsteps)
    def _():
        nxt = step + 1
        for cp in copies(nxt // ns, nxt % ns, 1 - slot):
            cp.start()

    for cp in copies(b, s, slot):
        cp.wait()

    qb = q_ref[0]
    q4s = [jnp.concatenate([qb[:, (g * GQA_REP + r) * HEAD_DIM:(g * GQA_REP + r + 1) * HEAD_DIM]
                            for r in range(GQA_REP)], axis=0) for g in range(N_KV_HEADS)]

    @pl.when(s == 0)
    def _():
        r_scr[:, ns * nch:ns * nch + 8, :] = jnp.zeros((4, 8, 256), F32)

    row0 = pl.multiple_of(s * nch, nch)
    for c in range(2):
        acc = jnp.zeros((N_KV_HEADS * nch, 256), F32)
        for j in range(8):
            def chunk_rows(l, g):
                return buf[slot, pl.ds(l * 8 + c * 2 + g, nch, stride=128), :]
            lhs = jnp.concatenate(
                [jnp.concatenate([chunk_rows(2 * j, g), chunk_rows(2 * j + 1, g)], axis=1)
                 for g in range(N_KV_HEADS)], axis=0).astype(BF16)
            acc = acc + jnp.dot(lhs, w_ref[c, j], preferred_element_type=F32)
        for g in range(N_KV_HEADS):
            r_scr[c * 2 + g, pl.ds(row0, nch), :] = acc[g * nch:(g + 1) * nch]

    key0 = pl.multiple_of(s * nk, nk)
    for g in range(N_KV_HEADS):
        k = buf[slot, pl.ds(4 + g, nk, stride=8), :].astype(BF16)
        sc_scr[g, s] = _nt(q4s[g], k)
        v_scr[g, pl.ds(key0, nk), :] = buf[slot, pl.ds(6 + g, nk, stride=8), :].astype(BF16)

    @pl.when(s == ns - 1)
    def _():
        t_q = past_len + lax.broadcasted_iota(jnp.int32, (nq, 1), 0)
        t_rows = jnp.concatenate([t_q] * GQA_REP, axis=0)
        gts_all = jax.nn.sigmoid(zs_ref[0])
        nz_all = nz_ref[0].astype(F32)
        cck = _compress_const(pe_ref, w_ref, 0)
        ccv = _compress_const(pe_ref, w_ref, 1)
        for g in range(N_KV_HEADS):
            q4 = q4s[g]
            kc = _combine_cmp(r_scr.at[g], cck, n, n_cmp).astype(BF16)
            vc = _combine_cmp(r_scr.at[2 + g], ccv, n, n_cmp).astype(BF16)
            o_cmp, sel = _cmp_and_select(q4, kc, vc, ov_ref[...], t_q, n, n_cmp, n_slc, n_sel, nsp)

            def chunk_mask(blk0, kpos0, nkeys):
                win0 = (blk0 // 128) * 128
                mq = _sel_mask(sel[:, win0:win0 + 128], win0, blk0, t_q, kpos0, nkeys)
                return jnp.concatenate([mq] * GQA_REP, axis=0)

            mx = jnp.full((nr, 1), -jnp.inf, F32)
            for cidx in range(ns):
                sm = jnp.where(chunk_mask(cidx * pp * 2, cidx * nk, nk), sc_scr[g, cidx], -jnp.inf)
                sc_scr[g, cidx] = sm
                mx = jnp.maximum(mx, jnp.max(sm, axis=-1, keepdims=True))
            s_new = jnp.where(chunk_mask(past_len // SLC_BLK, past_len, 128),
                              _nt(q4, new_ref[g, 0]), -jnp.inf)
            mx = jnp.maximum(mx, jnp.max(s_new, axis=-1, keepdims=True))
            mx = jnp.where(mx > -jnp.inf, mx, 0.0)
            p_new = jnp.exp(s_new - mx)
            den = jnp.sum(p_new, axis=-1, keepdims=True)
            acc = jnp.dot(p_new.astype(BF16), new_ref[2 + g, 0], preferred_element_type=F32)
            for cidx in range(ns):
                p = jnp.exp(sc_scr[g, cidx] - mx)
                den = den + jnp.sum(p, axis=-1, keepdims=True)
                acc = acc + jnp.dot(p.astype(BF16), v_scr[g, cidx * nk:(cidx + 1) * nk, :],
                                    preferred_element_type=F32)
            o_slc = acc / jnp.maximum(den, 1e-30)

            kw = cw_ref[0, 0, pl.ds(g, WINDOW, stride=4), :].astype(BF16)
            vw = cw_ref[0, 0, pl.ds(2 + g, WINDOW, stride=4), :].astype(BF16)
            d1 = t_rows - (past_len - WINDOW + lax.broadcasted_iota(jnp.int32, (1, WINDOW), 1))
            s1 = jnp.where((d1 >= 0) & (d1 < WINDOW), _nt(q4, kw), -jnp.inf)
            d2 = t_rows - (past_len + lax.broadcasted_iota(jnp.int32, (1, 128), 1))
            s2 = jnp.where((d2 >= 0) & (d2 < WINDOW), _nt(q4, new_ref[4 + g, 0]), -jnp.inf)
            mw = jnp.maximum(jnp.max(s1, axis=-1, keepdims=True), jnp.max(s2, axis=-1, keepdims=True))
            mw = jnp.where(mw > -jnp.inf, mw, 0.0)
            p1 = jnp.exp(s1 - mw)
            p2 = jnp.exp(s2 - mw)
            denw = jnp.sum(p1, axis=-1, keepdims=True) + jnp.sum(p2, axis=-1, keepdims=True)
            o_win = (jnp.dot(p1.astype(BF16), vw, preferred_element_type=F32)
                     + jnp.dot(p2.astype(BF16), new_ref[6 + g, 0], preferred_element_type=F32))
            o_win = o_win / jnp.maximum(denw, 1e-30)

            def write(r, val, g=g):
                c0 = (g * GQA_REP + r) * HEAD_DIM
                o_ref[0, :, c0:c0 + HEAD_DIM] = val.astype(o_ref.dtype)

            _gate_and_write(write, gts_all[:, 128 * (1 + g):128 * (2 + g)],
                            nz_all[:, g * 512:(g + 1) * 512], o_cmp, o_slc, o_win, nq)


def _nsa_decode(page_table, q_s, cache_pages, wcat, pe_lhs, new_att, cache_win_rows, ov, z_small, z_main, layer,
                past_len, nq, dec_t, pp):
    dec_b, n_pages = page_table.shape
    ns = n_pages // pp
    n = n_pages * 8
    tk = past_len + dec_t
    n_cmp = tk // CMP_STRIDE - CMP_LEN // CMP_STRIDE + 1
    n_slc = -(-tk // SLC_BLK)
    nsp = ov.shape[1]
    nr = GQA_REP * nq
    q3 = q_s.reshape(dec_b, nq, 1024)
    zs3 = z_small.reshape(dec_b, nq, N_SMALL)
    zm3 = z_main.reshape(dec_b, nq, N_MAIN)
    grid_spec = pltpu.PrefetchScalarGridSpec(
        num_scalar_prefetch=1,
        grid=(dec_b, ns),
        in_specs=[pl.BlockSpec(memory_space=pl.ANY),
                  pl.BlockSpec((1, nq, 1024), lambda b, s, pt: (b, 0, 0)),
                  pl.BlockSpec((2, 8, 256, 256), lambda b, s, pt: (0, 0, 0, 0)),
                  pl.BlockSpec((2, 8, 16, 256), lambda b, s, pt: (0, 0, 0, 0)),
                  pl.BlockSpec((8, 1, 128, HEAD_DIM), lambda b, s, pt: (0, b, 0, 0)),
                  pl.BlockSpec((1, 1, 4 * WINDOW, HEAD_DIM), lambda b, s, pt: (layer, b, 0, 0)),
                  pl.BlockSpec(ov.shape, lambda b, s, pt: (0, 0)),
                  pl.BlockSpec((1, nq, N_SMALL), lambda b, s, pt: (b, 0, 0)),
                  pl.BlockSpec((1, nq, 1024), lambda b, s, pt: (b, 0, C_NSAZ // 1024))],
        out_specs=pl.BlockSpec((1, nq, 1024), lambda b, s, pt: (b, 0, 0)),
        scratch_shapes=[pltpu.VMEM((2, pp * 1024, HEAD_DIM), F32),
                        pltpu.SemaphoreType.DMA((2,)),
                        pltpu.VMEM((4, n + 8, 256), F32),
                        pltpu.VMEM((N_KV_HEADS, ns, nr, pp * 128), F32),
                        pltpu.VMEM((N_KV_HEADS, n_pages * 128, HEAD_DIM), BF16)],
    )
    out = pl.pallas_call(
        functools.partial(_nsa_decode_kernel, layer=layer, pp=pp, ns=ns, n=n, n_cmp=n_cmp, n_slc=n_slc,
                          n_sel=min(N_SEL, n_slc), nsp=nsp, past_len=past_len, nq=nq),
        grid_spec=grid_spec,
        out_shape=jax.ShapeDtypeStruct((dec_b, nq, 1024), BF16),
        compiler_params=_cparams(("arbitrary", "arbitrary")),
        name="nsa_decode",
    )(page_table, cache_pages, q3, wcat, pe_lhs, new_att, cache_win_rows, ov, zs3, zm3)
    return out.reshape(dec_b * nq, 1024)


def _pool_kernel(u_ref, pz_ref, pre_ref, w_ref, sc_ref, o_ref, buf, *, tr, pos0):
    i = pl.program_id(1)

    @pl.when(i == 0)
    def _():
        buf[0:16, :] = pre_ref[0]

    u = u_ref[0].astype(F32)
    buf[16:16 + tr, :] = u
    pos = pos0 + i * tr + lax.broadcasted_iota(jnp.int32, (tr, 1), 0)
    for gi, w in enumerate(POOL_WINDOWS):
        cs = slice(gi * POOL_GROUP, (gi + 1) * POOL_GROUP)
        s = buf[16:16 + tr, cs]
        for k in range(1, w):
            s = s + buf[16 - k:16 - k + tr, cs]
        cnt = jnp.minimum(w, pos + 1).astype(F32)
        pooled = s / cnt - u[:, cs]
        mixed = jnp.dot(pooled.astype(BF16), w_ref[gi], preferred_element_type=F32) * sc_ref[:, cs]
        o_ref[0, :, cs] = (mixed * _silu(pz_ref[0, :, cs].astype(F32))).astype(o_ref.dtype)
    buf[0:16, :] = buf[tr:tr + 16, :]


def _pool(z_main, prefix16, pool_w, pool_scale, bsz, t, tr, pos0):
    zm3 = z_main.reshape(bsz, t, N_MAIN)
    out = pl.pallas_call(
        functools.partial(_pool_kernel, tr=tr, pos0=pos0),
        grid=(bsz, t // tr),
        in_specs=[pl.BlockSpec((1, tr, 1024), lambda b, i: (b, i, C_POOLU // 1024)),
                  pl.BlockSpec((1, tr, 1024), lambda b, i: (b, i, C_POOLZ // 1024)),
                  pl.BlockSpec((1, 16, 1024), lambda b, i: (b, 0, 0)),
                  pl.BlockSpec((4, POOL_GROUP, POOL_GROUP), lambda b, i: (0, 0, 0)),
                  pl.BlockSpec((1, 1024), lambda b, i: (0, 0))],
        out_specs=pl.BlockSpec((1, tr, 1024), lambda b, i: (b, i, 0)),
        out_shape=jax.ShapeDtypeStruct((bsz, t, 1024), BF16),
        scratch_shapes=[pltpu.VMEM((16 + tr, 1024), F32)],
        compiler_params=_cparams(("arbitrary", "arbitrary")),
        name="pool",
    )(zm3, zm3, prefix16, pool_w, pool_scale.reshape(1, 1024))
    return out.reshape(bsz * t, 1024)


def _ssd_kernel(xs_ref, bm_ref, cm_ref, dt_ref, mz_ref, pre_ref, init_ref, cw_ref, cb_ref, dtb_ref, alx_ref,
                alp_ref, dsk_ref, mn_ref, ex_ref, y_ref, fin_ref, xbuf, st, *, rows_in, valid_len):
    c = pl.program_id(1)
    nc = pl.num_programs(1)
    L = SSD_CHUNK

    @pl.when(c == 0)
    def _():
        xbuf[0:8, :] = pre_ref[0]
        for pr in range(8):
            st[:, pr * 128:(pr + 1) * 128] = init_ref[0, pr * 128:(pr + 1) * 128, :].T

    if rows_in < L:
        xbuf[8 + rows_in:8 + L, :] = jnp.zeros((L - rows_in, CONV_CH), F32)
    xbuf[8:8 + rows_in, 0:1024] = xs_ref[0].astype(F32)
    xbuf[8:8 + rows_in, 1024:1280] = bm_ref[0].astype(F32)
    xbuf[8:8 + rows_in, 1280:1536] = cm_ref[0].astype(F32)
    acc = cb_ref[...] + xbuf[5:5 + L, :] * cw_ref[0:1, :]
    for j in range(1, CONV_W):
        acc = acc + xbuf[5 + j:5 + j + L, :] * cw_ref[j:j + 1, :]
    xbuf[0:8, :] = xbuf[L:L + 8, :]
    xc = _silu(acc)
    xs = xc[:, 0:1024]
    bmat = xc[:, 1024:1280]
    cmat = xc[:, 1280:1536]

    if rows_in < L:
        dtz = jnp.concatenate([dt_ref[0], jnp.zeros((L - rows_in, 128), F32)], axis=0)
    else:
        dtz = dt_ref[0]
    dtz = dtz + dtb_ref[...]
    dt = jnp.maximum(dtz, 0.0) + jnp.log1p(jnp.exp(-jnp.abs(dtz)))
    row = c * L + lax.broadcasted_iota(jnp.int32, (L, 1), 0)
    dt = jnp.where(row < valid_len, dt, 0.0)

    li = lax.broadcasted_iota(jnp.int32, (L, L), 0)
    si = lax.broadcasted_iota(jnp.int32, (L, L), 1)
    tril_b = li >= si
    tril = tril_b.astype(F32)
    dt_x = jnp.dot(dt, ex_ref[...], precision=HI, preferred_element_type=F32)
    la_x = dt_x * (-jnp.exp(alx_ref[...]))
    acs_x = jnp.dot(tril, la_x, precision=HI, preferred_element_type=F32)
    la_p = dt * (-jnp.exp(alp_ref[...]))
    acs_pt = jnp.dot(tril, la_p, precision=HI, preferred_element_type=F32).T

    xdt = xs * dt_x
    a_end = acs_x[L - 1:L, :]
    eacs = jnp.exp(acs_x)
    xw = (xdt * jnp.exp(a_end - acs_x)).astype(BF16)
    xdt_b = xdt.astype(BF16)
    lane = lax.broadcasted_iota(jnp.int32, (1, 128), 1)
    y_parts = []
    for g in range(M_GROUPS):
        bg = bmat[:, g * M_STATE:(g + 1) * M_STATE]
        cg = cmat[:, g * M_STATE:(g + 1) * M_STATE].astype(BF16)
        gmat = _nt(cg, bg.astype(BF16))
        bgt = bg.T.astype(BF16)
        for hp in range(M_HEADS // M_GROUPS // 2):
            pr = g * (M_HEADS // M_GROUPS // 2) + hp
            cs = slice(pr * 128, (pr + 1) * 128)
            xpair = xdt_b[:, cs]
            ydiag = jnp.zeros((L, 128), F32)
            for half in range(2):
                h = 2 * pr + half
                col = acs_x[:, h * M_HEAD_DIM:h * M_HEAD_DIM + 1]
                rw = acs_pt[h:h + 1, :]
                dec = jnp.where(tril_b, jnp.exp(col - rw), 0.0)
                mm = (gmat * dec).astype(BF16)
                in_half = (lane >= half * M_HEAD_DIM) & (lane < (half + 1) * M_HEAD_DIM)
                rhs = jnp.where(in_half, xpair, jnp.zeros_like(xpair))
                ydiag = ydiag + jnp.dot(mm, rhs, preferred_element_type=F32)
            prev = st[:, cs]
            yoff = jnp.dot(cg, prev.astype(BF16), preferred_element_type=F32) * eacs[:, cs]
            st[:, cs] = jnp.exp(a_end[:, cs]) * prev + jnp.dot(bgt, xw[:, cs], preferred_element_type=F32)
            y_parts.append(ydiag + yoff)
    y = jnp.concatenate(y_parts, axis=1)
    y = y + dsk_ref[...] * xs
    if rows_in < L:
        mz = jnp.concatenate([mz_ref[0].astype(F32), jnp.zeros((L - rows_in, 1024), F32)], axis=0)
    else:
        mz = mz_ref[0].astype(F32)
    y = y * _silu(mz)
    half_w = BRANCH_WIDTH // M_GROUPS
    outs = []
    for g in range(M_GROUPS):
        yg = y[:, g * half_w:(g + 1) * half_w]
        outs.append(yg * lax.rsqrt(jnp.mean(yg * yg, axis=-1, keepdims=True) + EPS))
    yn = jnp.concatenate(outs, axis=1) * mn_ref[...]
    y_ref[0] = yn[0:rows_in].astype(y_ref.dtype)

    @pl.when(c == nc - 1)
    def _():
        for pr in range(8):
            fin_ref[0, pr * 128:(pr + 1) * 128, :] = st[:, pr * 128:(pr + 1) * 128].T


def _ssd(z_main, z_small, prefix8, init, consts, bsz, t, rows_in, valid_len):
    conv_w8, conv_b, dtb, alx, alp, dskx, mnorm, expand = consts
    zm3 = z_main.reshape(bsz, t, N_MAIN)
    zs3 = z_small.reshape(bsz, t, N_SMALL)
    nc = t // rows_in
    full = lambda shp: pl.BlockSpec(shp, lambda b, c: tuple(0 for _ in shp))
    y, fin = pl.pallas_call(
        functools.partial(_ssd_kernel, rows_in=rows_in, valid_len=valid_len),
        grid=(bsz, nc),
        in_specs=[pl.BlockSpec((1, rows_in, 1024), lambda b, c: (b, c, C_XS // 1024)),
                  pl.BlockSpec((1, rows_in, 256), lambda b, c: (b, c, C_B // 256)),
                  pl.BlockSpec((1, rows_in, 256), lambda b, c: (b, c, C_C // 256)),
                  pl.BlockSpec((1, rows_in, 128), lambda b, c: (b, c, 0)),
                  pl.BlockSpec((1, rows_in, 1024), lambda b, c: (b, c, C_MZ // 1024)),
                  pl.BlockSpec((1, 8, CONV_CH), lambda b, c: (b, 0, 0)),
                  pl.BlockSpec((1, 1024, M_STATE), lambda b, c: (b, 0, 0)),
                  full((8, CONV_CH)), full((1, CONV_CH)), full((1, 128)), full((1, 1024)), full((1, 128)),
                  full((1, 1024)), full((1, 1024)), full((128, 1024))],
        out_specs=[pl.BlockSpec((1, rows_in, 1024), lambda b, c: (b, c, 0)),
                   pl.BlockSpec((1, 1024, M_STATE), lambda b, c: (b, 0, 0))],
        out_shape=[jax.ShapeDtypeStruct((bsz, t, 1024), BF16),
                   jax.ShapeDtypeStruct((bsz, 1024, M_STATE), F32)],
        scratch_shapes=[pltpu.VMEM((8 + SSD_CHUNK, CONV_CH), F32), pltpu.VMEM((M_STATE, 1024), F32)],
        compiler_params=_cparams(("arbitrary", "arbitrary")),
        name="ssd",
    )(zm3, zm3, zm3, zs3, zm3, prefix8, init, conv_w8, conv_b, dtb, alx, alp, dskx, mnorm, expand)
    return y.reshape(bsz * t, 1024), fin


def _overlap_matrix(n, n_cmp, n_slc, nsp):
    i = np.arange(n)[:, None]
    j = np.arange(nsp)[None, :]
    ov = (i * CMP_STRIDE < j * SLC_BLK + SLC_BLK) & (i * CMP_STRIDE + CMP_LEN > j * SLC_BLK)
    ov = ov & (i < n_cmp) & (j < n_slc)
    return jnp.asarray(ov.astype(np.float32), dtype=BF16)


def _rope_tables(pos):
    inv = ROPE_THETA ** (-jnp.arange(0, HEAD_DIM, 2, dtype=F32) / HEAD_DIM)
    ang = pos.astype(F32)[:, None] * inv[None, :]
    cos, sin = jnp.cos(ang), jnp.sin(ang)
    return jnp.concatenate([cos, cos], axis=1), jnp.concatenate([-sin, sin], axis=1)


def _layer_consts(l, w_in, qk_gain, cmp_pe, cmp_w, pool_w, conv_w, conv_b, dt_bias, a_log, d_skip, mnorm_w,
                  w_branch, w_out):
    offs = np.cumsum((0, 1024, 1536, 24, 1024, 1024, 1024, 1024, CONV_CH, M_HEADS, 3 * D_MODEL))
    seg = lambda k: w_in[l][:, offs[k]:offs[k + 1]]
    xbc = seg(7)
    w_main = jnp.concatenate([seg(0), seg(3), seg(4), seg(5), seg(6), xbc[:, :1024], seg(9), seg(1),
                              xbc[:, 1024:1280], xbc[:, 1280:1536]], axis=1).astype(BF16)
    gates = seg(2)
    zpad = lambda w: jnp.zeros((D_MODEL, w), F32)
    w_small = jnp.concatenate([seg(8), zpad(128 - M_HEADS), gates[:, :12], zpad(116), gates[:, 12:], zpad(116)],
                              axis=1).astype(BF16)
    gain8 = jnp.concatenate([qk_gain[l], jnp.zeros((4, HEAD_DIM), F32)], axis=0)
    w4 = cmp_w[l].reshape(2, 2, 8, 2, HEAD_DIM, HEAD_DIM)
    wcat = jnp.transpose(w4, (0, 2, 3, 4, 1, 5)).reshape(2, 8, 256, 256).astype(BF16)
    pe4 = cmp_pe[l].reshape(2, 2, 8, 2 * HEAD_DIM)
    pe_lhs = jnp.concatenate([jnp.transpose(pe4, (0, 2, 1, 3)), jnp.zeros((2, 8, 14, 256), F32)], axis=2)
    rep = lambda v: jnp.repeat(v, M_HEAD_DIM).reshape(1, 1024)
    pad128 = lambda v: jnp.concatenate([v, jnp.zeros((128 - v.shape[0],), F32)]).reshape(1, 128)
    expand = (np.arange(128)[:, None] == (np.arange(1024)[None, :] // M_HEAD_DIM)).astype(np.float32)
    ssd_consts = (jnp.concatenate([conv_w[l], jnp.zeros((4, CONV_CH), F32)], axis=0), conv_b[l].reshape(1, CONV_CH),
                  pad128(dt_bias[l]), rep(a_log[l]), pad128(a_log[l]), rep(d_skip[l]), mnorm_w[l].reshape(1, 1024),
                  jnp.asarray(expand))
    return dict(w_main=w_main, w_small=w_small, gain8=gain8, wcat=wcat, pe_lhs=pe_lhs,
                pool_w=pool_w[l].astype(BF16), ssd=ssd_consts, w_branch=w_branch[l].astype(BF16),
                w_out=w_out[l].astype(BF16))


def _xbc_cols(z3):
    return jnp.concatenate([z3[..., C_XS:C_XS + 1024], z3[..., C_B:C_B + 256], z3[..., C_C:C_C + 256]], axis=-1)


def _tile(m, pref):
    for t in pref:
        if m % t == 0:
            return t
    return m


def kernel(x_prompt, x_sample, cache_kv, cache_win, state_pool, state_conv, state_ssm, page_table, norm_w, w_in,
           qk_gain, cmp_pe, cmp_w, pool_w, pool_scale, conv_w, conv_b, dt_bias, a_log, d_skip, mnorm_w, w_branch,
           w_out):
    depth = w_in.shape[0]
    bsz, seq, _ = x_prompt.shape
    dec_b, dec_t, _ = x_sample.shape
    n_pool, page = cache_kv.shape[1], cache_kv.shape[2]
    n_pages = page_table.shape[1]
    past_len = n_pages * page
    assert page == 128 and dec_t == 4 and cache_win.shape[2] == WINDOW and seq % 128 == 0
    nq = 8
    cache_pages = cache_kv.reshape(depth, n_pool, page * 4 * N_KV_HEADS, HEAD_DIM)
    cache_win_rows = cache_win.reshape(depth, dec_b, WINDOW * 2 * N_KV_HEADS, HEAD_DIM)
    tk_p = _tile(seq, (512, 256, 128))

    mp = bsz * seq
    ms = dec_b * nq
    xp = x_prompt.reshape(mp, D_MODEL)
    xs = jnp.pad(x_sample, ((0, 0), (0, nq - dec_t), (0, 0))).reshape(ms, D_MODEL)

    cos_p, sin_p = _rope_tables(jnp.tile(jnp.arange(seq), bsz))
    cos_s, sin_s = _rope_tables(jnp.tile(past_len + jnp.arange(nq), dec_b))

    n_p = seq // CMP_STRIDE
    n_slc_p = -(-seq // SLC_BLK)
    nsb_p = -(-n_slc_p // 16) * 16
    assert nsb_p <= 128
    ovt_p = _overlap_matrix(n_p, n_p - 1, n_slc_p, nsb_p).T
    n_s = n_pages * 8
    tk = past_len + dec_t
    n_slc_s = -(-tk // SLC_BLK)
    nsp_s = -(-n_slc_s // 128) * 128
    ov_s = _overlap_matrix(n_s, tk // CMP_STRIDE - 1, n_slc_s, nsp_s)

    tm_p = _tile(mp, (512, 256, 128))
    tm_big = _tile(mp, (1024, 512, 256, 128))
    outs_p = [[] for _ in range(3)]
    outs_s = [[] for _ in range(3)]
    rows_p = jnp.zeros((depth, mp * 8, HEAD_DIM), F32)
    win_p = jnp.zeros((depth, mp * 4, HEAD_DIM), F32)
    rows_s = jnp.zeros((depth, ms * 8, HEAD_DIM), F32)
    win_s = jnp.zeros((depth, ms * 4, HEAD_DIM), F32)
    hp = _rmsnorm(xp, norm_w[0], tm_p)
    hs = _rmsnorm(xs, norm_w[0], ms)
    for l in range(depth):
        c = _layer_consts(l, w_in, qk_gain, cmp_pe, cmp_w, pool_w, conv_w, conv_b, dt_bias, a_log, d_skip,
                          mnorm_w, w_branch, w_out)
        g_next = norm_w[min(l + 1, depth - 1)]
        z_main = _matmul(hp, c["w_main"], tm_big, 1024, "inproj", BF16)
        z_small = _matmul(hp, c["w_small"], tm_p, N_SMALL, "inproj_small")
        q_r, rows_p, win_p, cmpin, kv_att, v_t = _prep(z_main, cos_p, sin_p, c["gain8"], _tile(mp, (256, 128)),
                                                       True, l, depth, rows_p, win_p)
        r_cmp, c_cmp = _compress_prompt(cmpin, c["wcat"], c["pe_lhs"], bsz, seq)
        nsa_out = _nsa_prompt(q_r, r_cmp, c_cmp, kv_att, v_t, ovt_p, z_small, z_main, bsz, seq, tk_p)
        pool_out = _pool(z_main, jnp.zeros((bsz, 16, 1024), F32), c["pool_w"], pool_scale[l], bsz, seq,
                         _tile(seq, (256, 128)), 0)
        m_out, fin = _ssd(z_main, z_small, jnp.zeros((bsz, 8, CONV_CH), F32),
                          jnp.zeros((bsz, 1024, M_STATE), F32), c["ssd"], bsz, seq, SSD_CHUNK, seq)
        merged = _merge(nsa_out, pool_out, m_out, c["w_branch"], z_main, tm_big, 512)
        xp, hp = _outproj(xp, merged, c["w_out"], g_next, tm_p)
        zm3 = z_main.reshape(bsz, seq, N_MAIN)
        outs_p[0].append(zm3[:, seq - POOL_STATE:, C_POOLU:C_POOLU + 1024].astype(F32))
        outs_p[1].append(_xbc_cols(zm3[:, seq - (CONV_W - 1):]).astype(F32))
        outs_p[2].append(fin.reshape(bsz, M_HEADS, M_HEAD_DIM, M_STATE))
        z_main = _matmul(hs, c["w_main"], ms, 1024, "inproj_s")
        z_small = _matmul(hs, c["w_small"], ms, N_SMALL, "inproj_small_s")
        q_r, rows_s, win_s, _, kv_att = _prep(z_main, cos_s, sin_s, c["gain8"], ms, False, l, depth, rows_s, win_s)
        new_att = jnp.pad(kv_att.reshape(8, dec_b, nq, HEAD_DIM), ((0, 0), (0, 0), (0, 128 - nq), (0, 0)))
        new_att = jnp.where((jnp.arange(128) < dec_t)[None, None, :, None], new_att, jnp.zeros_like(new_att))
        nsa_out = _nsa_decode(page_table, q_r, cache_pages, c["wcat"], c["pe_lhs"], new_att, cache_win_rows, ov_s,
                              z_small, z_main, l, past_len, nq, dec_t, _tile(n_pages, (16, 8, 4, 2)))
        pre16 = jnp.pad(state_pool[l], ((0, 0), (16 - POOL_STATE, 0), (0, 0)))
        pool_out = _pool(z_main, pre16, c["pool_w"], pool_scale[l], dec_b, nq, nq, past_len)
        pre8 = jnp.pad(state_conv[l], ((0, 0), (8 - (CONV_W - 1), 0), (0, 0)))
        m_out, fin = _ssd(z_main, z_small, pre8, state_ssm[l].reshape(dec_b, 1024, M_STATE), c["ssd"], dec_b, nq,
                          nq, dec_t)
        merged = _merge(nsa_out, pool_out, m_out, c["w_branch"], z_main, ms, 1024)
        xs, hs = _outproj(xs, merged, c["w_out"], g_next, ms)
        zm3 = z_main.reshape(dec_b, nq, N_MAIN)
        outs_s[0].append(jnp.concatenate([state_pool[l], zm3[:, :dec_t, C_POOLU:C_POOLU + 1024]],
                                         axis=1)[:, -POOL_STATE:])
        outs_s[1].append(jnp.concatenate([state_conv[l], _xbc_cols(zm3[:, :dec_t])], axis=1)[:, -(CONV_W - 1):])
        outs_s[2].append(fin.reshape(dec_b, M_HEADS, M_HEAD_DIM, M_STATE))

    y_p = xp.reshape(bsz, seq, D_MODEL)
    y_s = xs.reshape(dec_b, nq, D_MODEL)[:, :dec_t]
    wlen = min(WINDOW, seq)
    kv_p = rows_p.reshape(depth, bsz, seq, 4, N_KV_HEADS, HEAD_DIM)
    win_p = win_p.reshape(depth, bsz, seq, 2, N_KV_HEADS, HEAD_DIM)[:, :, seq - wlen:]
    kv_s = rows_s.reshape(depth, dec_b, nq, 4, N_KV_HEADS, HEAD_DIM)[:, :, :dec_t]
    win_new = win_s.reshape(depth, dec_b, nq, 2, N_KV_HEADS, HEAD_DIM)[:, :, :dec_t]
    win_s = jnp.concatenate([cache_win, win_new], axis=2)[:, :, -WINDOW:]
    stk = lambda lst: jnp.stack(lst)
    return (y_p, y_s, kv_p, win_p, *[stk(v) for v in outs_p], kv_s, win_s, *[stk(v) for v in outs_s])
```

```python
import functools

import numpy as np
import jax
import jax.numpy as jnp
from jax import lax
from jax.experimental import pallas as pl
from jax.experimental.pallas import tpu as pltpu

F32 = jnp.float32
BF16 = jnp.bfloat16

D_MODEL = 2048
BRANCH_WIDTH = D_MODEL // 2
HEAD_DIM = 128
N_HEADS = BRANCH_WIDTH // HEAD_DIM
GQA_REP = 4
N_KV_HEADS = N_HEADS // GQA_REP
ROPE_THETA = 10000.0
CMP_LEN = 32
CMP_STRIDE = 16
SLC_BLK = 64
SLC_SHIFT = SLC_BLK.bit_length() - 1
N_SEL = 16
WINDOW = 512
Q_BLOCK = 128
SEL_BONUS = 1.0e4
POOL_WINDOWS = (2, 4, 8, 16)
POOL_GROUP = BRANCH_WIDTH // len(POOL_WINDOWS)
POOL_STATE = max(POOL_WINDOWS) - 1
M_HEAD_DIM = 64
M_HEADS = BRANCH_WIDTH // M_HEAD_DIM
M_STATE = 128
M_GROUPS = 2
CONV_W = 4
CONV_CH = BRANCH_WIDTH + 2 * M_GROUPS * M_STATE
SSD_CHUNK = 128
EPS = 1e-6
SCALE = HEAD_DIM ** -0.5
NEG_BIG = -1.0e30

C_Q = 0
C_NSAZ = 1024
C_POOLU = 2048
C_POOLZ = 3072
C_MZ = 4096
C_XS = 5120
C_MG = 6144
C_KV = 12288
C_B = 13824
C_C = 14080
N_MAIN = 14336
N_SMALL = 384

VMEM_LIMIT = 56 * 1024 * 1024
HI = lax.Precision.HIGHEST


def _cparams(sem):
    return pltpu.CompilerParams(dimension_semantics=sem, vmem_limit_bytes=VMEM_LIMIT)


def _nt(a, b):
    return lax.dot_general(a, b, (((1,), (1,)), ((), ())), preferred_element_type=F32)


def _silu(x):
    return x * jax.nn.sigmoid(x)


def _rmsnorm_kernel(x_ref, g_ref, o_ref):
    x = x_ref[...]
    ms = jnp.mean(x * x, axis=-1, keepdims=True)
    o_ref[...] = (x * lax.rsqrt(ms + EPS) * g_ref[...]).astype(o_ref.dtype)


def _rmsnorm(x, g, tr):
    m, d = x.shape
    return pl.pallas_call(
        _rmsnorm_kernel,
        grid=(m // tr,),
        in_specs=[pl.BlockSpec((tr, d), lambda i: (i, 0)), pl.BlockSpec((1, d), lambda i: (0, 0))],
        out_specs=pl.BlockSpec((tr, d), lambda i: (i, 0)),
        out_shape=jax.ShapeDtypeStruct((m, d), BF16),
        compiler_params=_cparams(("parallel",)),
        name="rmsnorm",
    )(x, g.reshape(1, d))


def _mm_kernel(a_ref, b_ref, o_ref):
    o_ref[...] = jnp.dot(a_ref[...], b_ref[...], preferred_element_type=F32).astype(o_ref.dtype)


def _matmul(a, b, tm, tn, name, out_dtype=F32):
    m, k = a.shape
    n = b.shape[1]
    return pl.pallas_call(
        _mm_kernel,
        grid=(m // tm, n // tn),
        in_specs=[pl.BlockSpec((tm, k), lambda i, j: (i, 0)), pl.BlockSpec((k, tn), lambda i, j: (0, j))],
        out_specs=pl.BlockSpec((tm, tn), lambda i, j: (i, j)),
        out_shape=jax.ShapeDtypeStruct((m, n), out_dtype),
        compiler_params=_cparams(("parallel", "parallel")),
        name=name,
    )(a, b)


def _merge_kernel(a_ref, p_ref, m_ref, w_ref, g0_ref, g1_ref, g2_ref, o_ref):
    gate = lambda g_ref: jax.nn.sigmoid(g_ref[...].astype(F32))
    acc = gate(g0_ref) * jnp.dot(a_ref[...], w_ref[0], preferred_element_type=F32)
    acc = acc + gate(g1_ref) * jnp.dot(p_ref[...], w_ref[1], preferred_element_type=F32)
    acc = acc + gate(g2_ref) * jnp.dot(m_ref[...], w_ref[2], preferred_element_type=F32)
    o_ref[...] = acc.astype(o_ref.dtype)


def _merge(nsa_out, pool_out, m_out, w_branch, z_main, tm, tn):
    m = nsa_out.shape[0]
    gblk = [(C_MG + b * D_MODEL) // tn for b in range(3)]
    act = pl.BlockSpec((tm, BRANCH_WIDTH), lambda i, j: (i, 0))
    return pl.pallas_call(
        _merge_kernel,
        grid=(m // tm, D_MODEL // tn),
        in_specs=[act, act, act,
                  pl.BlockSpec((3, BRANCH_WIDTH, tn), lambda i, j: (0, 0, j)),
                  pl.BlockSpec((tm, tn), lambda i, j: (i, gblk[0] + j)),
                  pl.BlockSpec((tm, tn), lambda i, j: (i, gblk[1] + j)),
                  pl.BlockSpec((tm, tn), lambda i, j: (i, gblk[2] + j))],
        out_specs=pl.BlockSpec((tm, tn), lambda i, j: (i, j)),
        out_shape=jax.ShapeDtypeStruct((m, D_MODEL), BF16),
        compiler_params=_cparams(("parallel", "parallel")),
        name="merge",
    )(nsa_out, pool_out, m_out, w_branch, z_main, z_main, z_main)


def _outproj_kernel(x_ref, a_ref, w_ref, g_ref, o_ref, h_ref):
    y = x_ref[...] + jnp.dot(a_ref[...], w_ref[...], preferred_element_type=F32)
    o_ref[...] = y
    ms = jnp.mean(y * y, axis=-1, keepdims=True)
    h_ref[...] = (y * lax.rsqrt(ms + EPS) * g_ref[...]).astype(h_ref.dtype)


def _outproj(x, merged, w_out, g_next, tm):
    m = x.shape[0]
    row = pl.BlockSpec((tm, D_MODEL), lambda i: (i, 0))
    return pl.pallas_call(
        _outproj_kernel,
        grid=(m // tm,),
        in_specs=[row, row, pl.BlockSpec((D_MODEL, D_MODEL), lambda i: (0, 0)),
                  pl.BlockSpec((1, D_MODEL), lambda i: (0, 0))],
        out_specs=[row, row],
        out_shape=[jax.ShapeDtypeStruct((m, D_MODEL), F32), jax.ShapeDtypeStruct((m, D_MODEL), BF16)],
        compiler_params=_cparams(("parallel",)),
        name="outproj",
    )(x, merged, w_out, g_next.reshape(1, D_MODEL))


def _prep_kernel(zq_ref, zkv_ref, cos_ref, sin_ref, gain_ref, rows_in_ref, win_in_ref,
                 q_ref, rows_ref, win_ref, cmpin_ref, kvatt_ref, *maybe_vt_ref, tr):
    del rows_in_ref, win_in_ref
    cos = cos_ref[...]
    sin = sin_ref[...]

    def norm_rope(x, gi):
        ms = jnp.mean(x * x, axis=-1, keepdims=True)
        y = x * lax.rsqrt(ms + EPS) * gain_ref[gi:gi + 1, :]
        return y * cos + pltpu.roll(y, HEAD_DIM // 2, 1) * sin

    for h in range(N_HEADS):
        sl = slice(h * HEAD_DIM, (h + 1) * HEAD_DIM)
        q_ref[:, sl] = (norm_rope(zq_ref[:, sl].astype(F32), 0) * SCALE).astype(q_ref.dtype)
    for br in range(3):
        for g in range(N_KV_HEADS):
            ck = br * 512 + g * HEAD_DIM
            cv = br * 512 + 256 + g * HEAD_DIM
            k = norm_rope(zkv_ref[:, ck:ck + HEAD_DIM].astype(F32), 1 + br)
            v = zkv_ref[:, cv:cv + HEAD_DIM].astype(F32)
            if br < 2:
                rows_ref[0, pl.ds(br * 4 + g, tr, stride=8), :] = k
                rows_ref[0, pl.ds(br * 4 + 2 + g, tr, stride=8), :] = v
            else:
                win_ref[0, pl.ds(g, tr, stride=4), :] = k
                win_ref[0, pl.ds(2 + g, tr, stride=4), :] = v
            if br == 0:
                cmpin_ref[g] = k
                cmpin_ref[2 + g] = v
            else:
                base = (br - 1) * 4
                kvatt_ref[base + g] = k.astype(kvatt_ref.dtype)
                kvatt_ref[base + 2 + g] = v.astype(kvatt_ref.dtype)
                if maybe_vt_ref:
                    for j in range(tr // 128):
                        maybe_vt_ref[0][(br - 1) * 2 + g, j] = v[j * 128:(j + 1) * 128].T.astype(kvatt_ref.dtype)


def _prep(z_main, cos2, sin2, gain8, tr, emit_vt, layer, depth, rows_all, win_all):
    m = z_main.shape[0]
    vt_specs = [pl.BlockSpec((4, tr // 128, HEAD_DIM, 128), lambda i: (0, i, 0, 0))] if emit_vt else []
    vt_shapes = [jax.ShapeDtypeStruct((4, m // 128, HEAD_DIM, 128), BF16)] if emit_vt else []
    return pl.pallas_call(
        functools.partial(_prep_kernel, tr=tr),
        grid=(m // tr,),
        in_specs=[pl.BlockSpec((tr, 1024), lambda i: (i, C_Q // 1024)),
                  pl.BlockSpec((tr, 1536), lambda i: (i, C_KV // 1536)),
                  pl.BlockSpec((tr, HEAD_DIM), lambda i: (i, 0)),
                  pl.BlockSpec((tr, HEAD_DIM), lambda i: (i, 0)),
                  pl.BlockSpec((8, HEAD_DIM), lambda i: (0, 0)),
                  pl.BlockSpec(memory_space=pl.ANY), pl.BlockSpec(memory_space=pl.ANY)],
        out_specs=[pl.BlockSpec((tr, 1024), lambda i: (i, 0)),
                   pl.BlockSpec((1, tr * 8, HEAD_DIM), lambda i: (layer, i, 0)),
                   pl.BlockSpec((1, tr * 4, HEAD_DIM), lambda i: (layer, i, 0)),
                   pl.BlockSpec((4, tr, HEAD_DIM), lambda i: (0, i, 0)),
                   pl.BlockSpec((8, tr, HEAD_DIM), lambda i: (0, i, 0))] + vt_specs,
        out_shape=[jax.ShapeDtypeStruct((m, 1024), BF16),
                   jax.ShapeDtypeStruct((depth, m * 8, HEAD_DIM), F32),
                   jax.ShapeDtypeStruct((depth, m * 4, HEAD_DIM), F32),
                   jax.ShapeDtypeStruct((4, m, HEAD_DIM), F32),
                   jax.ShapeDtypeStruct((8, m, HEAD_DIM), BF16)] + vt_shapes,
        input_output_aliases={5: 1, 6: 2},
        compiler_params=_cparams(("parallel",)),
        name="nsa_prep",
    )(z_main, z_main, cos2, sin2, gain8, rows_all, win_all)


def _compress_rows(load_pair, w_ref, c, n):
    acc = jnp.zeros((n, 256), F32)
    for j in range(8):
        xa, xb = load_pair(j)
        lhs = jnp.concatenate([xa, xb], axis=1).astype(BF16)
        acc = acc + jnp.dot(lhs, w_ref[c, j], preferred_element_type=F32)
    return acc


def _compress_const(pe_ref, w_ref, c):
    acc = jnp.zeros((16, 256), F32)
    for j in range(8):
        acc = acc + jnp.dot(pe_ref[c, j].astype(BF16), w_ref[c, j], preferred_element_type=F32)
    return acc[0:1, 0:128] + acc[1:2, 128:256]


def _compress_prompt_kernel(x_ref, w_ref, pe_ref, r_ref, c_ref, *, n):
    c = pl.program_id(1) // 2

    def load_pair(j):
        return (x_ref[0, 0, pl.ds(2 * j, n, stride=16), :],
                x_ref[0, 0, pl.ds(2 * j + 1, n, stride=16), :])

    r_ref[0, 0, 0:n, :] = _compress_rows(load_pair, w_ref, c, n)
    r_ref[0, 0, n:n + 8, :] = jnp.zeros((8, 256), F32)
    c_ref[0, 0] = jnp.broadcast_to(_compress_const(pe_ref, w_ref, c), (8, 128))


def _compress_prompt(cmpin, wcat, pe_lhs, bsz, t):
    n = t // CMP_STRIDE
    x = cmpin.reshape(4, bsz, t, HEAD_DIM)
    return pl.pallas_call(
        functools.partial(_compress_prompt_kernel, n=n),
        grid=(bsz, 4),
        in_specs=[pl.BlockSpec((1, 1, t, HEAD_DIM), lambda b, cg: (cg, b, 0, 0)),
                  pl.BlockSpec((2, 8, 256, 256), lambda b, cg: (0, 0, 0, 0)),
                  pl.BlockSpec((2, 8, 16, 256), lambda b, cg: (0, 0, 0, 0))],
        out_specs=[pl.BlockSpec((1, 1, n + 8, 256), lambda b, cg: (b, cg, 0, 0)),
                   pl.BlockSpec((1, 1, 8, 128), lambda b, cg: (b, cg, 0, 0))],
        out_shape=[jax.ShapeDtypeStruct((bsz, 4, n + 8, 256), F32),
                   jax.ShapeDtypeStruct((bsz, 4, 8, 128), F32)],
        compiler_params=_cparams(("parallel", "parallel")),
        name="compress_prompt",
    )(x, wcat, pe_lhs)


def _combine_cmp(r, cc, n, n_cmp):
    a = r[0:n, 0:128]
    bsh = r[1:n + 1, 128:256]
    ci = lax.broadcasted_iota(jnp.int32, (n, 1), 0)
    return jnp.where(ci < n_cmp, a + bsh + cc, 0.0)


def _softmax_rows(s, mask):
    s = jnp.where(mask, s, -jnp.inf)
    m = jnp.max(s, axis=-1, keepdims=True)
    m = jnp.where(m > -jnp.inf, m, 0.0)
    e = jnp.exp(s - m)
    return e / jnp.maximum(jnp.sum(e, axis=-1, keepdims=True), 1e-30)


def _cmp_and_select(q4, kc, vc, ov, t_q, n, n_cmp, n_slc, n_sel, nsp):
    nq = t_q.shape[0]
    t_rows = jnp.concatenate([t_q] * GQA_REP, axis=0)
    sc = _nt(q4, kc)
    ci = lax.broadcasted_iota(jnp.int32, (1, n), 1)
    cmask = (ci * CMP_STRIDE + (CMP_LEN - 1) <= t_rows) & (ci < n_cmp)
    pc = _softmax_rows(sc, cmask)
    o_cmp = jnp.dot(pc.astype(BF16), vc, preferred_element_type=F32)
    pcs = pc[0:nq] + pc[nq:2 * nq] + pc[2 * nq:3 * nq] + pc[3 * nq:4 * nq]
    hi = pcs.astype(BF16)
    lo = (pcs - hi.astype(F32)).astype(BF16)
    imp = jnp.dot(hi, ov, preferred_element_type=F32) + jnp.dot(lo, ov, preferred_element_type=F32)
    j = lax.broadcasted_iota(jnp.int32, (1, nsp), 1)
    cur = t_q >> SLC_SHIFT
    forced = (j == 0) | (j == cur) | (j == cur - 1)
    valid = (j * SLC_BLK <= t_q) & (j < n_slc)
    score = jnp.where(valid, imp + SEL_BONUS * forced.astype(F32), -jnp.inf)
    rank = jnp.zeros((nq, nsp), F32)
    for k in range(n_slc):
        col = score[:, k:k + 1]
        before = (col > score) | ((col == score) & (k < j))
        rank = rank + before.astype(F32)
    sel = (rank < float(n_sel)) & (j < n_slc)
    return o_cmp, sel.astype(BF16)


def _sel_mask(sel_win, win0, blk0, t_q, kpos0, nk):
    jb = win0 + lax.broadcasted_iota(jnp.int32, (128, nk), 0)
    kk = lax.broadcasted_iota(jnp.int32, (128, nk), 1)
    expand = (jb == blk0 + (kk >> SLC_SHIFT)).astype(BF16)
    hit = jnp.dot(sel_win, expand, preferred_element_type=F32)
    kpos = kpos0 + lax.broadcasted_iota(jnp.int32, (1, nk), 1)
    return (hit > 0.5) & (kpos <= t_q)


def _gate_and_write(o_ref_write, gts, nz, o_cmp, o_slc, o_win, nq):
    for r in range(GQA_REP):
        rs = slice(r * nq, (r + 1) * nq)
        o = (gts[:, 3 * r:3 * r + 1] * o_cmp[rs] + gts[:, 3 * r + 1:3 * r + 2] * o_slc[rs]
             + gts[:, 3 * r + 2:3 * r + 3] * o_win[rs])
        o_ref_write(r, o * _silu(nz[:, r * HEAD_DIM:(r + 1) * HEAD_DIM]))


def _nsa_prompt_kernel(q_ref, rk_ref, rv_ref, ck_ref, cv_ref, ks_ref, vst_ref, kw_ref, vwt_ref, ovt_ref,
                       gz_ref, nz_ref, o_ref, *, n, n_cmp, n_slc, n_sel, nsb, tk, nwt):
    i = pl.program_id(2)
    nq = Q_BLOCK
    nr = GQA_REP * nq
    qb = q_ref[0]
    q4t = jnp.concatenate([qb[:, r * HEAD_DIM:(r + 1) * HEAD_DIM].astype(F32).T.astype(BF16)
                           for r in range(GQA_REP)], axis=1)
    t_q = i * nq + lax.broadcasted_iota(jnp.int32, (1, nq), 1)
    t_q4 = jnp.concatenate([t_q] * GQA_REP, axis=1)

    kc = _combine_cmp(rk_ref.at[0, 0], ck_ref[0, 0, 0:1, :], n, n_cmp).astype(BF16)
    vct = _combine_cmp(rv_ref.at[0, 0], cv_ref[0, 0, 0:1, :], n, n_cmp).T.astype(BF16)
    sc = jnp.dot(kc, q4t, preferred_element_type=F32)
    ci = lax.broadcasted_iota(jnp.int32, (n, 1), 0)
    cmask = (ci * CMP_STRIDE + (CMP_LEN - 1) <= t_q4) & (ci < n_cmp)
    sc = jnp.where(cmask, sc, -jnp.inf)
    mx = jnp.max(sc, axis=0, keepdims=True)
    mx = jnp.where(mx > -jnp.inf, mx, 0.0)
    e = jnp.exp(sc - mx)
    pc = e * (1.0 / jnp.maximum(jnp.sum(e, axis=0, keepdims=True), 1e-30))
    o_cmp = jnp.dot(vct, pc.astype(BF16), preferred_element_type=F32)
    pcs = pc[:, 0:nq] + pc[:, nq:2 * nq] + pc[:, 2 * nq:3 * nq] + pc[:, 3 * nq:4 * nq]
    hi = pcs.astype(BF16)
    lo = (pcs - hi.astype(F32)).astype(BF16)
    ovt = ovt_ref[...]
    imp = jnp.dot(ovt, hi, preferred_element_type=F32) + jnp.dot(ovt, lo, preferred_element_type=F32)

    jb = lax.broadcasted_iota(jnp.int32, (nsb, 1), 0)
    cur = t_q >> SLC_SHIFT
    forced = (jb == 0) | (jb == cur) | (jb == cur - 1)
    valid = (jb * SLC_BLK <= t_q) & (jb < n_slc)
    score = jnp.where(valid, imp + SEL_BONUS * forced.astype(F32), -jnp.inf)
    rank = jnp.zeros((nsb, nq), F32)
    for k in range(n_slc):
        row = score[k:k + 1, :]
        before = (row > score) | ((row == score) & (k < jb))
        rank = rank + before.astype(F32)
    sel = ((rank < float(n_sel)) & (jb < n_slc)).astype(BF16)
    if nsb < 128:
        sel = jnp.concatenate([sel, jnp.zeros((128 - nsb, nq), BF16)], axis=0)

    def masked_scores(k, mq):
        st = jnp.dot(k, q4t, preferred_element_type=F32)
        return jnp.concatenate([jnp.where(mq, st[:, r * nq:(r + 1) * nq], -jnp.inf)
                                for r in range(GQA_REP)], axis=1)

    def keys_at(k_ref, vt_ref, tile0, ntiles):
        k = k_ref[0, 0, pl.ds(pl.multiple_of(tile0 * 128, 128), ntiles * 128), :]
        vt = jnp.concatenate([vt_ref[0, 0, tile0 + j] for j in range(ntiles)], axis=1)
        kpos = tile0 * 128 + lax.broadcasted_iota(jnp.int32, (ntiles * 128, 1), 0)
        return k, vt, kpos

    tpt = tk // 128

    def slc_body(kb, carry):
        m, l, acc = carry
        k, vt, kpos = keys_at(ks_ref, vst_ref, kb * tpt, tpt)
        jl = lax.broadcasted_iota(jnp.int32, (tk, 128), 1)
        expand = (jl == (kpos >> SLC_SHIFT)).astype(BF16)
        hit = jnp.dot(expand, sel, preferred_element_type=F32)
        st = masked_scores(k, (hit > 0.5) & (kpos <= t_q))
        m_new = jnp.maximum(m, jnp.max(st, axis=0, keepdims=True))
        p = jnp.exp(st - m_new)
        alpha = jnp.exp(m - m_new)
        l_new = alpha * l + jnp.sum(p, axis=0, keepdims=True)
        acc_new = alpha * acc + jnp.dot(vt, p.astype(BF16), preferred_element_type=F32)
        return m_new, l_new, acc_new

    init = (jnp.full((1, nr), NEG_BIG, F32), jnp.zeros((1, nr), F32), jnp.zeros((HEAD_DIM, nr), F32))
    m, l, acc = lax.fori_loop(0, (i * nq + nq - 1) // tk + 1, slc_body, init)
    o_slc = acc * (1.0 / jnp.maximum(l, 1e-30))

    k, vt, kpos = keys_at(kw_ref, vwt_ref, jnp.maximum(i + 1 - nwt, 0), nwt)
    d = t_q - kpos
    st = masked_scores(k, (d >= 0) & (d < WINDOW))
    mw = jnp.max(st, axis=0, keepdims=True)
    mw = jnp.where(mw > -jnp.inf, mw, 0.0)
    p = jnp.exp(st - mw)
    o_win = (jnp.dot(vt, p.astype(BF16), preferred_element_type=F32)
             * (1.0 / jnp.maximum(jnp.sum(p, axis=0, keepdims=True), 1e-30)))

    gt = jax.nn.sigmoid(gz_ref[0]).T
    nz = nz_ref[0].astype(F32)
    for r in range(GQA_REP):
        cs = slice(r * nq, (r + 1) * nq)
        ot = (gt[3 * r:3 * r + 1, :] * o_cmp[:, cs] + gt[3 * r + 1:3 * r + 2, :] * o_slc[:, cs]
              + gt[3 * r + 2:3 * r + 3, :] * o_win[:, cs])
        hs = slice(r * HEAD_DIM, (r + 1) * HEAD_DIM)
        o_ref[0, :, hs] = (ot.T * _silu(nz[:, hs])).astype(o_ref.dtype)


def _nsa_prompt(q_r, r_cmp, c_cmp, kv_att, v_t, ovt, z_small, z_main, bsz, t, tk):
    n = t // CMP_STRIDE
    n_cmp = n - CMP_LEN // CMP_STRIDE + 1
    n_slc = -(-t // SLC_BLK)
    nsb = ovt.shape[0]
    q3 = q_r.reshape(bsz, t, 1024)
    kv4 = kv_att.reshape(8, bsz, t, HEAD_DIM)
    vt5 = v_t.reshape(4, bsz, t // 128, HEAD_DIM, 128)
    nwt = min(WINDOW // Q_BLOCK + 1, t // 128)
    zs3 = z_small.reshape(bsz, t, N_SMALL)
    zm3 = z_main.reshape(bsz, t, N_MAIN)
    rspec = lambda off: pl.BlockSpec((1, 1, n + 8, 256), lambda b, g, i: (b, off + g, 0, 0))
    cspec = lambda off: pl.BlockSpec((1, 1, 8, 128), lambda b, g, i: (b, off + g, 0, 0))
    kspec = lambda off: pl.BlockSpec((1, 1, t, HEAD_DIM), lambda b, g, i: (off + g, b, 0, 0))
    vtspec = lambda off: pl.BlockSpec((1, 1, t // 128, HEAD_DIM, 128), lambda b, g, i: (off + g, b, 0, 0, 0))
    out = pl.pallas_call(
        functools.partial(_nsa_prompt_kernel, n=n, n_cmp=n_cmp, n_slc=n_slc, n_sel=min(N_SEL, n_slc), nsb=nsb,
                          tk=tk, nwt=nwt),
        grid=(bsz, N_KV_HEADS, t // Q_BLOCK),
        in_specs=[pl.BlockSpec((1, Q_BLOCK, 512), lambda b, g, i: (b, i, g)),
                  rspec(0), rspec(2), cspec(0), cspec(2),
                  kspec(0), vtspec(0), kspec(4), vtspec(2),
                  pl.BlockSpec(ovt.shape, lambda b, g, i: (0, 0)),
                  pl.BlockSpec((1, Q_BLOCK, 128), lambda b, g, i: (b, i, 1 + g)),
                  pl.BlockSpec((1, Q_BLOCK, 512), lambda b, g, i: (b, i, C_NSAZ // 512 + g))],
        out_specs=pl.BlockSpec((1, Q_BLOCK, 512), lambda b, g, i: (b, i, g)),
        out_shape=jax.ShapeDtypeStruct((bsz, t, 1024), BF16),
        compiler_params=_cparams(("parallel", "parallel", "arbitrary")),
        name="nsa_prompt",
    )(q3, r_cmp, r_cmp, c_cmp, c_cmp, kv4, vt5, kv4, vt5, ovt, zs3, zm3)
    return out.reshape(bsz * t, 1024)


def _nsa_decode_kernel(pt_ref, cache_ref, q_ref, w_ref, pe_ref, new_ref, cw_ref, ov_ref, zs_ref, nz_ref, o_ref,
                       buf, sem, r_scr, sc_scr, v_scr,
                       *, layer, pp, ns, n, n_cmp, n_slc, n_sel, nsp, past_len, nq):
    b = pl.program_id(0)
    s = pl.program_id(1)
    step = b * ns + s
    nsteps = pl.num_programs(0) * ns
    slot = step % 2
    nch = pp * 8
    nk = pp * 128
    nr = GQA_REP * nq

    def copies(bb, ss, sl):
        return [pltpu.make_async_copy(cache_ref.at[layer, pt_ref[bb, ss * pp + j]],
                                      buf.at[sl, pl.ds(j * 1024, 1024), :], sem.at[sl]) for j in range(pp)]

    @pl.when(step == 0)
    def _():
        for cp in copies(b, s, slot):
            cp.start()

    @pl.when(step + 1 < nsteps)
    def _():
        nxt = step + 1
        for cp in copies(nxt // ns, nxt % ns, 1 - slot):
            cp.start()

    for cp in copies(b, s, slot):
        cp.wait()

    qb = q_ref[0]
    q4s = [jnp.concatenate([qb[:, (g * GQA_REP + r) * HEAD_DIM:(g * GQA_REP + r + 1) * HEAD_DIM]
                            for r in range(GQA_REP)], axis=0) for g in range(N_KV_HEADS)]

    @pl.when(s == 0)
    def _():
        r_scr[:, ns * nch:ns * nch + 8, :] = jnp.zeros((4, 8, 256), F32)

    row0 = pl.multiple_of(s * nch, nch)
    for c in range(2):
        acc = jnp.zeros((N_KV_HEADS * nch, 256), F32)
        for j in range(8):
            def chunk_rows(l, g):
                return buf[slot, pl.ds(l * 8 + c * 2 + g, nch, stride=128), :]
            lhs = jnp.concatenate(
                [jnp.concatenate([chunk_rows(2 * j, g), chunk_rows(2 * j + 1, g)], axis=1)
                 for g in range(N_KV_HEADS)], axis=0).astype(BF16)
            acc = acc + jnp.dot(lhs, w_ref[c, j], preferred_element_type=F32)
        for g in range(N_KV_HEADS):
            r_scr[c * 2 + g, pl.ds(row0, nch), :] = acc[g * nch:(g + 1) * nch]

    key0 = pl.multiple_of(s * nk, nk)
    for g in range(N_KV_HEADS):
        k = buf[slot, pl.ds(4 + g, nk, stride=8), :].astype(BF16)
        sc_scr[g, s] = _nt(q4s[g], k)
        v_scr[g, pl.ds(key0, nk), :] = buf[slot, pl.ds(6 + g, nk, stride=8), :].astype(BF16)

    @pl.when(s == ns - 1)
    def _():
        t_q = past_len + lax.broadcasted_iota(jnp.int32, (nq, 1), 0)
        t_rows = jnp.concatenate([t_q] * GQA_REP, axis=0)
        gts_all = jax.nn.sigmoid(zs_ref[0])
        nz_all = nz_ref[0].astype(F32)
        cck = _compress_const(pe_ref, w_ref, 0)
        ccv = _compress_const(pe_ref, w_ref, 1)
        for g in range(N_KV_HEADS):
            q4 = q4s[g]
            kc = _combine_cmp(r_scr.at[g], cck, n, n_cmp).astype(BF16)
            vc = _combine_cmp(r_scr.at[2 + g], ccv, n, n_cmp).astype(BF16)
            o_cmp, sel = _cmp_and_select(q4, kc, vc, ov_ref[...], t_q, n, n_cmp, n_slc, n_sel, nsp)

            def chunk_mask(blk0, kpos0, nkeys):
                win0 = (blk0 // 128) * 128
                mq = _sel_mask(sel[:, win0:win0 + 128], win0, blk0, t_q, kpos0, nkeys)
                return jnp.concatenate([mq] * GQA_REP, axis=0)

            mx = jnp.full((nr, 1), -jnp.inf, F32)
            for cidx in range(ns):
                sm = jnp.where(chunk_mask(cidx * pp * 2, cidx * nk, nk), sc_scr[g, cidx], -jnp.inf)
                sc_scr[g, cidx] = sm
                mx = jnp.maximum(mx, jnp.max(sm, axis=-1, keepdims=True))
            s_new = jnp.where(chunk_mask(past_len // SLC_BLK, past_len, 128),
                              _nt(q4, new_ref[g, 0]), -jnp.inf)
            mx = jnp.maximum(mx, jnp.max(s_new, axis=-1, keepdims=True))
            mx = jnp.where(mx > -jnp.inf, mx, 0.0)
            p_new = jnp.exp(s_new - mx)
            den = jnp.sum(p_new, axis=-1, keepdims=True)
            acc = jnp.dot(p_new.astype(BF16), new_ref[2 + g, 0], preferred_element_type=F32)
            for cidx in range(ns):
                p = jnp.exp(sc_scr[g, cidx] - mx)
                den = den + jnp.sum(p, axis=-1, keepdims=True)
                acc = acc + jnp.dot(p.astype(BF16), v_scr[g, cidx * nk:(cidx + 1) * nk, :],
                                    preferred_element_type=F32)
            o_slc = acc / jnp.maximum(den, 1e-30)

            kw = cw_ref[0, 0, pl.ds(g, WINDOW, stride=4), :].astype(BF16)
            vw = cw_ref[0, 0, pl.ds(2 + g, WINDOW, stride=4), :].astype(BF16)
            d1 = t_rows - (past_len - WINDOW + lax.broadcasted_iota(jnp.int32, (1, WINDOW), 1))
            s1 = jnp.where((d1 >= 0) & (d1 < WINDOW), _nt(q4, kw), -jnp.inf)
            d2 = t_rows - (past_len + lax.broadcasted_iota(jnp.int32, (1, 128), 1))
            s2 = jnp.where((d2 >= 0) & (d2 < WINDOW), _nt(q4, new_ref[4 + g, 0]), -jnp.inf)
            mw = jnp.maximum(jnp.max(s1, axis=-1, keepdims=True), jnp.max(s2, axis=-1, keepdims=True))
            mw = jnp.where(mw > -jnp.inf, mw, 0.0)
            p1 = jnp.exp(s1 - mw)
            p2 = jnp.exp(s2 - mw)
            denw = jnp.sum(p1, axis=-1, keepdims=True) + jnp.sum(p2, axis=-1, keepdims=True)
            o_win = (jnp.dot(p1.astype(BF16), vw, preferred_element_type=F32)
                     + jnp.dot(p2.astype(BF16), new_ref[6 + g, 0], preferred_element_type=F32))
            o_win = o_win / jnp.maximum(denw, 1e-30)

            def write(r, val, g=g):
                c0 = (g * GQA_REP + r) * HEAD_DIM
                o_ref[0, :, c0:c0 + HEAD_DIM] = val.astype(o_ref.dtype)

            _gate_and_write(write, gts_all[:, 128 * (1 + g):128 * (2 + g)],
                            nz_all[:, g * 512:(g + 1) * 512], o_cmp, o_slc, o_win, nq)


def _nsa_decode(page_table, q_s, cache_pages, wcat, pe_lhs, new_att, cache_win_rows, ov, z_small, z_main, layer,
                past_len, nq, dec_t, pp):
    dec_b, n_pages = page_table.shape
    ns = n_pages // pp
    n = n_pages * 8
    tk = past_len + dec_t
    n_cmp = tk // CMP_STRIDE - CMP_LEN // CMP_STRIDE + 1
    n_slc = -(-tk // SLC_BLK)
    nsp = ov.shape[1]
    nr = GQA_REP * nq
    q3 = q_s.reshape(dec_b, nq, 1024)
    zs3 = z_small.reshape(dec_b, nq, N_SMALL)
    zm3 = z_main.reshape(dec_b, nq, N_MAIN)
    grid_spec = pltpu.PrefetchScalarGridSpec(
        num_scalar_prefetch=1,
        grid=(dec_b, ns),
        in_specs=[pl.BlockSpec(memory_space=pl.ANY),
                  pl.BlockSpec((1, nq, 1024), lambda b, s, pt: (b, 0, 0)),
                  pl.BlockSpec((2, 8, 256, 256), lambda b, s, pt: (0, 0, 0, 0)),
                  pl.BlockSpec((2, 8, 16, 256), lambda b, s, pt: (0, 0, 0, 0)),
                  pl.BlockSpec((8, 1, 128, HEAD_DIM), lambda b, s, pt: (0, b, 0, 0)),
                  pl.BlockSpec((1, 1, 4 * WINDOW, HEAD_DIM), lambda b, s, pt: (layer, b, 0, 0)),
                  pl.BlockSpec(ov.shape, lambda b, s, pt: (0, 0)),
                  pl.BlockSpec((1, nq, N_SMALL), lambda b, s, pt: (b, 0, 0)),
                  pl.BlockSpec((1, nq, 1024), lambda b, s, pt: (b, 0, C_NSAZ // 1024))],
        out_specs=pl.BlockSpec((1, nq, 1024), lambda b, s, pt: (b, 0, 0)),
        scratch_shapes=[pltpu.VMEM((2, pp * 1024, HEAD_DIM), F32),
                        pltpu.SemaphoreType.DMA((2,)),
                        pltpu.VMEM((4, n + 8, 256), F32),
                        pltpu.VMEM((N_KV_HEADS, ns, nr, pp * 128), F32),
                        pltpu.VMEM((N_KV_HEADS, n_pages * 128, HEAD_DIM), BF16)],
    )
    out = pl.pallas_call(
        functools.partial(_nsa_decode_kernel, layer=layer, pp=pp, ns=ns, n=n, n_cmp=n_cmp, n_slc=n_slc,
                          n_sel=min(N_SEL, n_slc), nsp=nsp, past_len=past_len, nq=nq),
        grid_spec=grid_spec,
        out_shape=jax.ShapeDtypeStruct((dec_b, nq, 1024), BF16),
        compiler_params=_cparams(("arbitrary", "arbitrary")),
        name="nsa_decode",
    )(page_table, cache_pages, q3, wcat, pe_lhs, new_att, cache_win_rows, ov, zs3, zm3)
    return out.reshape(dec_b * nq, 1024)


def _pool_kernel(u_ref, pz_ref, pre_ref, w_ref, sc_ref, o_ref, buf, *, tr, pos0):
    i = pl.program_id(1)

    @pl.when(i == 0)
    def _():
        buf[0:16, :] = pre_ref[0]

    u = u_ref[0].astype(F32)
    buf[16:16 + tr, :] = u
    pos = pos0 + i * tr + lax.broadcasted_iota(jnp.int32, (tr, 1), 0)
    for gi, w in enumerate(POOL_WINDOWS):
        cs = slice(gi * POOL_GROUP, (gi + 1) * POOL_GROUP)
        s = buf[16:16 + tr, cs]
        for k in range(1, w):
            s = s + buf[16 - k:16 - k + tr, cs]
        cnt = jnp.minimum(w, pos + 1).astype(F32)
        pooled = s / cnt - u[:, cs]
        mixed = jnp.dot(pooled.astype(BF16), w_ref[gi], preferred_element_type=F32) * sc_ref[:, cs]
        o_ref[0, :, cs] = (mixed * _silu(pz_ref[0, :, cs].astype(F32))).astype(o_ref.dtype)
    buf[0:16, :] = buf[tr:tr + 16, :]


def _pool(z_main, prefix16, pool_w, pool_scale, bsz, t, tr, pos0):
    zm3 = z_main.reshape(bsz, t, N_MAIN)
    out = pl.pallas_call(
        functools.partial(_pool_kernel, tr=tr, pos0=pos0),
        grid=(bsz, t // tr),
        in_specs=[pl.BlockSpec((1, tr, 1024), lambda b, i: (b, i, C_POOLU // 1024)),
                  pl.BlockSpec((1, tr, 1024), lambda b, i: (b, i, C_POOLZ // 1024)),
                  pl.BlockSpec((1, 16, 1024), lambda b, i: (b, 0, 0)),
                  pl.BlockSpec((4, POOL_GROUP, POOL_GROUP), lambda b, i: (0, 0, 0)),
                  pl.BlockSpec((1, 1024), lambda b, i: (0, 0))],
        out_specs=pl.BlockSpec((1, tr, 1024), lambda b, i: (b, i, 0)),
        out_shape=jax.ShapeDtypeStruct((bsz, t, 1024), BF16),
        scratch_shapes=[pltpu.VMEM((16 + tr, 1024), F32)],
        compiler_params=_cparams(("arbitrary", "arbitrary")),
        name="pool",
    )(zm3, zm3, prefix16, pool_w, pool_scale.reshape(1, 1024))
    return out.reshape(bsz * t, 1024)


def _ssd_kernel(xs_ref, bm_ref, cm_ref, dt_ref, mz_ref, pre_ref, init_ref, cw_ref, cb_ref, dtb_ref,
                alp_ref, dsk_ref, mn_ref, ex_ref, y_ref, fin_ref, xbuf, st, *, rows_in, valid_len):
    c = pl.program_id(1)
    nc = pl.num_programs(1)
    L = SSD_CHUNK

    @pl.when(c == 0)
    def _():
        xbuf[0:8, :] = pre_ref[0]
        for pr in range(8):
            st[:, pr * 128:(pr + 1) * 128] = init_ref[0, pr * 128:(pr + 1) * 128, :].T

    if rows_in < L:
        xbuf[8 + rows_in:8 + L, :] = jnp.zeros((L - rows_in, CONV_CH), F32)
    xbuf[8:8 + rows_in, 0:1024] = xs_ref[0].astype(F32)
    xbuf[8:8 + rows_in, 1024:1280] = bm_ref[0].astype(F32)
    xbuf[8:8 + rows_in, 1280:1536] = cm_ref[0].astype(F32)
    acc = cb_ref[...] + xbuf[5:5 + L, :] * cw_ref[0:1, :]
    for j in range(1, CONV_W):
        acc = acc + xbuf[5 + j:5 + j + L, :] * cw_ref[j:j + 1, :]
    xbuf[0:8, :] = xbuf[L:L + 8, :]
    xc = _silu(acc)
    xs = xc[:, 0:1024]
    bmat = xc[:, 1024:1280]
    cmat = xc[:, 1280:1536]

    if rows_in < L:
        dtz = jnp.concatenate([dt_ref[0], jnp.zeros((L - rows_in, 128), F32)], axis=0)
    else:
        dtz = dt_ref[0]
    dtz = dtz + dtb_ref[...]
    dt = jnp.maximum(dtz, 0.0) + jnp.log1p(jnp.exp(-jnp.abs(dtz)))
    row = c * L + lax.broadcasted_iota(jnp.int32, (L, 1), 0)
    dt = jnp.where(row < valid_len, dt, 0.0)

    li = lax.broadcasted_iota(jnp.int32, (L, L), 0)
    si = lax.broadcasted_iota(jnp.int32, (L, L), 1)
    tril_b = li >= si
    tril = tril_b.astype(F32)
    def per_channel(v):
        hi = v.astype(BF16)
        r1 = v - hi.astype(F32)
        mid = r1.astype(BF16)
        lo = (r1 - mid.astype(F32)).astype(BF16)
        e = ex_ref[...]
        return (jnp.dot(hi, e, preferred_element_type=F32) + jnp.dot(mid, e, preferred_element_type=F32)
                + jnp.dot(lo, e, preferred_element_type=F32))

    dt_x = per_channel(dt)
    la_p = dt * (-jnp.exp(alp_ref[...]))
    acs_p = jnp.dot(tril, la_p, precision=HI, preferred_element_type=F32)
    acs_pt = acs_p.T
    acs_x = per_channel(acs_p)

    xdt = xs * dt_x
    a_end = acs_x[L - 1:L, :]
    eacs = jnp.exp(acs_x)
    xw = (xdt * jnp.exp(a_end - acs_x)).astype(BF16)
    xdt_b = xdt.astype(BF16)
    lane = lax.broadcasted_iota(jnp.int32, (1, 128), 1)
    y_parts = []
    for g in range(M_GROUPS):
        bg = bmat[:, g * M_STATE:(g + 1) * M_STATE]
        cg = cmat[:, g * M_STATE:(g + 1) * M_STATE].astype(BF16)
        bgt = bg.T.astype(BF16)
        gmat = jnp.dot(cg, bgt, preferred_element_type=F32)
        for hp in range(M_HEADS // M_GROUPS // 2):
            pr = g * (M_HEADS // M_GROUPS // 2) + hp
            cs = slice(pr * 128, (pr + 1) * 128)
            xpair = xdt_b[:, cs]
            ydiag = jnp.zeros((L, 128), F32)
            for half in range(2):
                h = 2 * pr + half
                col = acs_x[:, h * M_HEAD_DIM:h * M_HEAD_DIM + 1]
                rw = acs_pt[h:h + 1, :]
                dec = jnp.where(tril_b, jnp.exp(col - rw), 0.0)
                mm = (gmat * dec).astype(BF16)
                in_half = (lane >= half * M_HEAD_DIM) & (lane < (half + 1) * M_HEAD_DIM)
                rhs = jnp.where(in_half, xpair, jnp.zeros_like(xpair))
                ydiag = ydiag + jnp.dot(mm, rhs, preferred_element_type=F32)
            prev = st[:, cs]
            yoff = jnp.dot(cg, prev.astype(BF16), preferred_element_type=F32) * eacs[:, cs]
            st[:, cs] = jnp.exp(a_end[:, cs]) * prev + jnp.dot(bgt, xw[:, cs], preferred_element_type=F32)
            y_parts.append(ydiag + yoff)
    y = jnp.concatenate(y_parts, axis=1)
    y = y + dsk_ref[...] * xs
    if rows_in < L:
        mz = jnp.concatenate([mz_ref[0].astype(F32), jnp.zeros((L - rows_in, 1024), F32)], axis=0)
    else:
        mz = mz_ref[0].astype(F32)
    y = y * _silu(mz)
    half_w = BRANCH_WIDTH // M_GROUPS
    outs = []
    for g in range(M_GROUPS):
        yg = y[:, g * half_w:(g + 1) * half_w]
        outs.append(yg * lax.rsqrt(jnp.mean(yg * yg, axis=-1, keepdims=True) + EPS))
    yn = jnp.concatenate(outs, axis=1) * mn_ref[...]
    y_ref[0] = yn[0:rows_in].astype(y_ref.dtype)

    @pl.when(c == nc - 1)
    def _():
        for pr in range(8):
            fin_ref[0, pr * 128:(pr + 1) * 128, :] = st[:, pr * 128:(pr + 1) * 128].T


def _ssd(z_main, z_small, prefix8, init, consts, bsz, t, rows_in, valid_len):
    conv_w8, conv_b, dtb, alp, dskx, mnorm, expand = consts
    zm3 = z_main.reshape(bsz, t, N_MAIN)
    zs3 = z_small.reshape(bsz, t, N_SMALL)
    nc = t // rows_in
    full = lambda shp: pl.BlockSpec(shp, lambda b, c: tuple(0 for _ in shp))
    y, fin = pl.pallas_call(
        functools.partial(_ssd_kernel, rows_in=rows_in, valid_len=valid_len),
        grid=(bsz, nc),
        in_specs=[pl.BlockSpec((1, rows_in, 1024), lambda b, c: (b, c, C_XS // 1024)),
                  pl.BlockSpec((1, rows_in, 256), lambda b, c: (b, c, C_B // 256)),
                  pl.BlockSpec((1, rows_in, 256), lambda b, c: (b, c, C_C // 256)),
                  pl.BlockSpec((1, rows_in, 128), lambda b, c: (b, c, 0)),
                  pl.BlockSpec((1, rows_in, 1024), lambda b, c: (b, c, C_MZ // 1024)),
                  pl.BlockSpec((1, 8, CONV_CH), lambda b, c: (b, 0, 0)),
                  pl.BlockSpec((1, 1024, M_STATE), lambda b, c: (b, 0, 0)),
                  full((8, CONV_CH)), full((1, CONV_CH)), full((1, 128)), full((1, 128)),
                  full((1, 1024)), full((1, 1024)), full((128, 1024))],
        out_specs=[pl.BlockSpec((1, rows_in, 1024), lambda b, c: (b, c, 0)),
                   pl.BlockSpec((1, 1024, M_STATE), lambda b, c: (b, 0, 0))],
        out_shape=[jax.ShapeDtypeStruct((bsz, t, 1024), BF16),
                   jax.ShapeDtypeStruct((bsz, 1024, M_STATE), F32)],
        scratch_shapes=[pltpu.VMEM((8 + SSD_CHUNK, CONV_CH), F32), pltpu.VMEM((M_STATE, 1024), F32)],
        compiler_params=_cparams(("arbitrary", "arbitrary")),
        name="ssd",
    )(zm3, zm3, zm3, zs3, zm3, prefix8, init, conv_w8, conv_b, dtb, alp, dskx, mnorm, expand)
    return y.reshape(bsz * t, 1024), fin


def _overlap_matrix(n, n_cmp, n_slc, nsp):
    i = np.arange(n)[:, None]
    j = np.arange(nsp)[None, :]
    ov = (i * CMP_STRIDE < j * SLC_BLK + SLC_BLK) & (i * CMP_STRIDE + CMP_LEN > j * SLC_BLK)
    ov = ov & (i < n_cmp) & (j < n_slc)
    return jnp.asarray(ov.astype(np.float32), dtype=BF16)


def _rope_tables(pos):
    inv = ROPE_THETA ** (-jnp.arange(0, HEAD_DIM, 2, dtype=F32) / HEAD_DIM)
    ang = pos.astype(F32)[:, None] * inv[None, :]
    cos, sin = jnp.cos(ang), jnp.sin(ang)
    return jnp.concatenate([cos, cos], axis=1), jnp.concatenate([-sin, sin], axis=1)


def _layer_consts(l, w_in, qk_gain, cmp_pe, cmp_w, pool_w, conv_w, conv_b, dt_bias, a_log, d_skip, mnorm_w,
                  w_branch, w_out):
    offs = np.cumsum((0, 1024, 1536, 24, 1024, 1024, 1024, 1024, CONV_CH, M_HEADS, 3 * D_MODEL))
    seg = lambda k: w_in[l][:, offs[k]:offs[k + 1]]
    xbc = seg(7)
    w_main = jnp.concatenate([seg(0), seg(3), seg(4), seg(5), seg(6), xbc[:, :1024], seg(9), seg(1),
                              xbc[:, 1024:1280], xbc[:, 1280:1536]], axis=1).astype(BF16)
    gates = seg(2)
    zpad = lambda w: jnp.zeros((D_MODEL, w), F32)
    w_small = jnp.concatenate([seg(8), zpad(128 - M_HEADS), gates[:, :12], zpad(116), gates[:, 12:], zpad(116)],
                              axis=1).astype(BF16)
    gain8 = jnp.concatenate([qk_gain[l], jnp.zeros((4, HEAD_DIM), F32)], axis=0)
    w4 = cmp_w[l].reshape(2, 2, 8, 2, HEAD_DIM, HEAD_DIM)
    wcat = jnp.transpose(w4, (0, 2, 3, 4, 1, 5)).reshape(2, 8, 256, 256).astype(BF16)
    pe4 = cmp_pe[l].reshape(2, 2, 8, 2 * HEAD_DIM)
    pe_lhs = jnp.concatenate([jnp.transpose(pe4, (0, 2, 1, 3)), jnp.zeros((2, 8, 14, 256), F32)], axis=2)
    rep = lambda v: jnp.repeat(v, M_HEAD_DIM).reshape(1, 1024)
    pad128 = lambda v: jnp.concatenate([v, jnp.zeros((128 - v.shape[0],), F32)]).reshape(1, 128)
    expand = (np.arange(128)[:, None] == (np.arange(1024)[None, :] // M_HEAD_DIM)).astype(np.float32)
    ssd_consts = (jnp.concatenate([conv_w[l], jnp.zeros((4, CONV_CH), F32)], axis=0), conv_b[l].reshape(1, CONV_CH),
                  pad128(dt_bias[l]), pad128(a_log[l]), rep(d_skip[l]), mnorm_w[l].reshape(1, 1024),
                  jnp.asarray(expand, dtype=BF16))
    return dict(w_main=w_main, w_small=w_small, gain8=gain8, wcat=wcat, pe_lhs=pe_lhs,
                pool_w=pool_w[l].astype(BF16), ssd=ssd_consts, w_branch=w_branch[l].astype(BF16),
                w_out=w_out[l].astype(BF16))


def _xbc_cols(z3):
    return jnp.concatenate([z3[..., C_XS:C_XS + 1024], z3[..., C_B:C_B + 256], z3[..., C_C:C_C + 256]], axis=-1)


def _tile(m, pref):
    for t in pref:
        if m % t == 0:
            return t
    return m


def kernel(x_prompt, x_sample, cache_kv, cache_win, state_pool, state_conv, state_ssm, page_table, norm_w, w_in,
           qk_gain, cmp_pe, cmp_w, pool_w, pool_scale, conv_w, conv_b, dt_bias, a_log, d_skip, mnorm_w, w_branch,
           w_out):
    depth = w_in.shape[0]
    bsz, seq, _ = x_prompt.shape
    dec_b, dec_t, _ = x_sample.shape
    n_pool, page = cache_kv.shape[1], cache_kv.shape[2]
    n_pages = page_table.shape[1]
    past_len = n_pages * page
    assert page == 128 and dec_t == 4 and cache_win.shape[2] == WINDOW and seq % 128 == 0
    nq = 8
    cache_pages = cache_kv.reshape(depth, n_pool, page * 4 * N_KV_HEADS, HEAD_DIM)
    cache_win_rows = cache_win.reshape(depth, dec_b, WINDOW * 2 * N_KV_HEADS, HEAD_DIM)
    tk_p = _tile(seq, (1024, 512, 256, 128))

    mp = bsz * seq
    ms = dec_b * nq
    xp = x_prompt.reshape(mp, D_MODEL)
    xs = jnp.pad(x_sample, ((0, 0), (0, nq - dec_t), (0, 0))).reshape(ms, D_MODEL)

    cos_p, sin_p = _rope_tables(jnp.tile(jnp.arange(seq), bsz))
    cos_s, sin_s = _rope_tables(jnp.tile(past_len + jnp.arange(nq), dec_b))

    n_p = seq // CMP_STRIDE
    n_slc_p = -(-seq // SLC_BLK)
    nsb_p = -(-n_slc_p // 16) * 16
    assert nsb_p <= 128
    ovt_p = _overlap_matrix(n_p, n_p - 1, n_slc_p, nsb_p).T
    n_s = n_pages * 8
    tk = past_len + dec_t
    n_slc_s = -(-tk // SLC_BLK)
    nsp_s = -(-n_slc_s // 128) * 128
    ov_s = _overlap_matrix(n_s, tk // CMP_STRIDE - 1, n_slc_s, nsp_s)

    tm_p = _tile(mp, (512, 256, 128))
    tm_big = _tile(mp, (1024, 512, 256, 128))
    outs_p = [[] for _ in range(3)]
    outs_s = [[] for _ in range(3)]
    rows_p = jnp.zeros((depth, mp * 8, HEAD_DIM), F32)
    win_p = jnp.zeros((depth, mp * 4, HEAD_DIM), F32)
    rows_s = jnp.zeros((depth, ms * 8, HEAD_DIM), F32)
    win_s = jnp.zeros((depth, ms * 4, HEAD_DIM), F32)
    hp = _rmsnorm(xp, norm_w[0], tm_p)
    hs = _rmsnorm(xs, norm_w[0], ms)
    for l in range(depth):
        c = _layer_consts(l, w_in, qk_gain, cmp_pe, cmp_w, pool_w, conv_w, conv_b, dt_bias, a_log, d_skip,
                          mnorm_w, w_branch, w_out)
        g_next = norm_w[min(l + 1, depth - 1)]
        z_main = _matmul(hp, c["w_main"], tm_big, 1024, "inproj", BF16)
        z_small = _matmul(hp, c["w_small"], tm_p, N_SMALL, "inproj_small")
        q_r, rows_p, win_p, cmpin, kv_att, v_t = _prep(z_main, cos_p, sin_p, c["gain8"], _tile(mp, (256, 128)),
                                                       True, l, depth, rows_p, win_p)
        r_cmp, c_cmp = _compress_prompt(cmpin, c["wcat"], c["pe_lhs"], bsz, seq)
        nsa_out = _nsa_prompt(q_r, r_cmp, c_cmp, kv_att, v_t, ovt_p, z_small, z_main, bsz, seq, tk_p)
        pool_out = _pool(z_main, jnp.zeros((bsz, 16, 1024), F32), c["pool_w"], pool_scale[l], bsz, seq,
                         _tile(seq, (256, 128)), 0)
        m_out, fin = _ssd(z_main, z_small, jnp.zeros((bsz, 8, CONV_CH), F32),
                          jnp.zeros((bsz, 1024, M_STATE), F32), c["ssd"], bsz, seq, SSD_CHUNK, seq)
        merged = _merge(nsa_out, pool_out, m_out, c["w_branch"], z_main, tm_big, 512)
        xp, hp = _outproj(xp, merged, c["w_out"], g_next, tm_p)
        zm3 = z_main.reshape(bsz, seq, N_MAIN)
        outs_p[0].append(zm3[:, seq - POOL_STATE:, C_POOLU:C_POOLU + 1024].astype(F32))
        outs_p[1].append(_xbc_cols(zm3[:, seq - (CONV_W - 1):]).astype(F32))
        outs_p[2].append(fin.reshape(bsz, M_HEADS, M_HEAD_DIM, M_STATE))
        z_main = _matmul(hs, c["w_main"], ms, 1024, "inproj_s")
        z_small = _matmul(hs, c["w_small"], ms, N_SMALL, "inproj_small_s")
        q_r, rows_s, win_s, _, kv_att = _prep(z_main, cos_s, sin_s, c["gain8"], ms, False, l, depth, rows_s, win_s)
        new_att = jnp.pad(kv_att.reshape(8, dec_b, nq, HEAD_DIM), ((0, 0), (0, 0), (0, 128 - nq), (0, 0)))
        new_att = jnp.where((jnp.arange(128) < dec_t)[None, None, :, None], new_att, jnp.zeros_like(new_att))
        nsa_out = _nsa_decode(page_table, q_r, cache_pages, c["wcat"], c["pe_lhs"], new_att, cache_win_rows, ov_s,
                              z_small, z_main, l, past_len, nq, dec_t, _tile(n_pages, (16, 8, 4, 2)))
        pre16 = jnp.pad(state_pool[l], ((0, 0), (16 - POOL_STATE, 0), (0, 0)))
        pool_out = _pool(z_main, pre16, c["pool_w"], pool_scale[l], dec_b, nq, nq, past_len)
        pre8 = jnp.pad(state_conv[l], ((0, 0), (8 - (CONV_W - 1), 0), (0, 0)))
        m_out, fin = _ssd(z_main, z_small, pre8, state_ssm[l].reshape(dec_b, 1024, M_STATE), c["ssd"], dec_b, nq,
                          nq, dec_t)
        merged = _merge(nsa_out, pool_out, m_out, c["w_branch"], z_main, ms, 1024)
        xs, hs = _outproj(xs, merged, c["w_out"], g_next, ms)
        zm3 = z_main.reshape(dec_b, nq, N_MAIN)
        outs_s[0].append(jnp.concatenate([state_pool[l], zm3[:, :dec_t, C_POOLU:C_POOLU + 1024]],
                                         axis=1)[:, -POOL_STATE:])
        outs_s[1].append(jnp.concatenate([state_conv[l], _xbc_cols(zm3[:, :dec_t])], axis=1)[:, -(CONV_W - 1):])
        outs_s[2].append(fin.reshape(dec_b, M_HEADS, M_HEAD_DIM, M_STATE))

    y_p = xp.reshape(bsz, seq, D_MODEL)
    y_s = xs.reshape(dec_b, nq, D_MODEL)[:, :dec_t]
    wlen = min(WINDOW, seq)
    kv_p = rows_p.reshape(depth, bsz, seq, 4, N_KV_HEADS, HEAD_DIM)
    win_p = win_p.reshape(depth, bsz, seq, 2, N_KV_HEADS, HEAD_DIM)[:, :, seq - wlen:]
    kv_s = rows_s.reshape(depth, dec_b, nq, 4, N_KV_HEADS, HEAD_DIM)[:, :, :dec_t]
    win_new = win_s.reshape(depth, dec_b, nq, 2, N_KV_HEADS, HEAD_DIM)[:, :, :dec_t]
    win_s = jnp.concatenate([cache_win, win_new], axis=2)[:, :, -WINDOW:]
    stk = lambda lst: jnp.stack(lst)
    return (y_p, y_s, kv_p, win_p, *[stk(v) for v in outs_p], kv_s, win_s, *[stk(v) for v in outs_s])
```

```python
import functools

import numpy as np
import jax
import jax.numpy as jnp
from jax import lax
from jax.experimental import pallas as pl
from jax.experimental.pallas import tpu as pltpu

F32 = jnp.float32
BF16 = jnp.bfloat16

D_MODEL = 2048
BRANCH_WIDTH = D_MODEL // 2
HEAD_DIM = 128
N_HEADS = BRANCH_WIDTH // HEAD_DIM
GQA_REP = 4
N_KV_HEADS = N_HEADS // GQA_REP
ROPE_THETA = 10000.0
CMP_LEN = 32
CMP_STRIDE = 16
SLC_BLK = 64
SLC_SHIFT = SLC_BLK.bit_length() - 1
N_SEL = 16
WINDOW = 512
Q_BLOCK = 128
SEL_BONUS = 1.0e4
POOL_WINDOWS = (2, 4, 8, 16)
POOL_GROUP = BRANCH_WIDTH // len(POOL_WINDOWS)
POOL_STATE = max(POOL_WINDOWS) - 1
M_HEAD_DIM = 64
M_HEADS = BRANCH_WIDTH // M_HEAD_DIM
M_STATE = 128
M_GROUPS = 2
CONV_W = 4
CONV_CH = BRANCH_WIDTH + 2 * M_GROUPS * M_STATE
SSD_CHUNK = 128
EPS = 1e-6
SCALE = HEAD_DIM ** -0.5
NEG_BIG = -1.0e30

C_Q = 0
C_NSAZ = 1024
C_POOLU = 2048
C_POOLZ = 3072
C_MZ = 4096
C_XS = 5120
C_MG = 6144
C_KV = 12288
C_B = 13824
C_C = 14080
N_MAIN = 14336
N_SMALL = 384

VMEM_LIMIT = 56 * 1024 * 1024
HI = lax.Precision.HIGHEST


def _cparams(sem):
    return pltpu.CompilerParams(dimension_semantics=sem, vmem_limit_bytes=VMEM_LIMIT)


def _nt(a, b):
    return lax.dot_general(a, b, (((1,), (1,)), ((), ())), preferred_element_type=F32)


def _silu(x):
    return x * jax.nn.sigmoid(x)


def _rmsnorm_kernel(x_ref, g_ref, o_ref):
    x = x_ref[...]
    ms = jnp.mean(x * x, axis=-1, keepdims=True)
    o_ref[...] = (x * lax.rsqrt(ms + EPS) * g_ref[...]).astype(o_ref.dtype)


def _rmsnorm(x, g, tr):
    m, d = x.shape
    return pl.pallas_call(
        _rmsnorm_kernel,
        grid=(m // tr,),
        in_specs=[pl.BlockSpec((tr, d), lambda i: (i, 0)), pl.BlockSpec((1, d), lambda i: (0, 0))],
        out_specs=pl.BlockSpec((tr, d), lambda i: (i, 0)),
        out_shape=jax.ShapeDtypeStruct((m, d), BF16),
        compiler_params=_cparams(("parallel",)),
        name="rmsnorm",
    )(x, g.reshape(1, d))


def _mm_kernel(a_ref, b_ref, o_ref):
    o_ref[...] = jnp.dot(a_ref[...], b_ref[...], preferred_element_type=F32).astype(o_ref.dtype)


def _matmul(a, b, tm, tn, name, out_dtype=F32):
    m, k = a.shape
    n = b.shape[1]
    return pl.pallas_call(
        _mm_kernel,
        grid=(m // tm, n // tn),
        in_specs=[pl.BlockSpec((tm, k), lambda i, j: (i, 0)), pl.BlockSpec((k, tn), lambda i, j: (0, j))],
        out_specs=pl.BlockSpec((tm, tn), lambda i, j: (i, j)),
        out_shape=jax.ShapeDtypeStruct((m, n), out_dtype),
        compiler_params=_cparams(("parallel", "parallel")),
        name=name,
    )(a, b)


def _merge_kernel(a_ref, p_ref, m_ref, w_ref, g0_ref, g1_ref, g2_ref, o_ref):
    gate = lambda g_ref: jax.nn.sigmoid(g_ref[...].astype(F32))
    acc = gate(g0_ref) * jnp.dot(a_ref[...], w_ref[0], preferred_element_type=F32)
    acc = acc + gate(g1_ref) * jnp.dot(p_ref[...], w_ref[1], preferred_element_type=F32)
    acc = acc + gate(g2_ref) * jnp.dot(m_ref[...], w_ref[2], preferred_element_type=F32)
    o_ref[...] = acc.astype(o_ref.dtype)


def _merge(nsa_out, pool_out, m_out, w_branch, z_main, tm, tn):
    m = nsa_out.shape[0]
    gblk = [(C_MG + b * D_MODEL) // tn for b in range(3)]
    act = pl.BlockSpec((tm, BRANCH_WIDTH), lambda i, j: (i, 0))
    return pl.pallas_call(
        _merge_kernel,
        grid=(m // tm, D_MODEL // tn),
        in_specs=[act, act, act,
                  pl.BlockSpec((3, BRANCH_WIDTH, tn), lambda i, j: (0, 0, j)),
                  pl.BlockSpec((tm, tn), lambda i, j: (i, gblk[0] + j)),
                  pl.BlockSpec((tm, tn), lambda i, j: (i, gblk[1] + j)),
                  pl.BlockSpec((tm, tn), lambda i, j: (i, gblk[2] + j))],
        out_specs=pl.BlockSpec((tm, tn), lambda i, j: (i, j)),
        out_shape=jax.ShapeDtypeStruct((m, D_MODEL), BF16),
        compiler_params=_cparams(("parallel", "parallel")),
        name="merge",
    )(nsa_out, pool_out, m_out, w_branch, z_main, z_main, z_main)


def _outproj_kernel(x_ref, a_ref, w_ref, g_ref, o_ref, h_ref):
    y = x_ref[...] + jnp.dot(a_ref[...], w_ref[...], preferred_element_type=F32)
    o_ref[...] = y
    ms = jnp.mean(y * y, axis=-1, keepdims=True)
    h_ref[...] = (y * lax.rsqrt(ms + EPS) * g_ref[...]).astype(h_ref.dtype)


def _outproj(x, merged, w_out, g_next, tm):
    m = x.shape[0]
    row = pl.BlockSpec((tm, D_MODEL), lambda i: (i, 0))
    return pl.pallas_call(
        _outproj_kernel,
        grid=(m // tm,),
        in_specs=[row, row, pl.BlockSpec((D_MODEL, D_MODEL), lambda i: (0, 0)),
                  pl.BlockSpec((1, D_MODEL), lambda i: (0, 0))],
        out_specs=[row, row],
        out_shape=[jax.ShapeDtypeStruct((m, D_MODEL), F32), jax.ShapeDtypeStruct((m, D_MODEL), BF16)],
        compiler_params=_cparams(("parallel",)),
        name="outproj",
    )(x, merged, w_out, g_next.reshape(1, D_MODEL))


def _prep_kernel(zq_ref, zkv_ref, cos_ref, sin_ref, gain_ref, rows_in_ref, win_in_ref,
                 q_ref, rows_ref, win_ref, cmpin_ref, kvatt_ref, *maybe_vt_ref, tr):
    del rows_in_ref, win_in_ref
    cos = cos_ref[...]
    sin = sin_ref[...]

    def norm_rope(x, gi):
        ms = jnp.mean(x * x, axis=-1, keepdims=True)
        y = x * lax.rsqrt(ms + EPS) * gain_ref[gi:gi + 1, :]
        return y * cos + pltpu.roll(y, HEAD_DIM // 2, 1) * sin

    for h in range(N_HEADS):
        sl = slice(h * HEAD_DIM, (h + 1) * HEAD_DIM)
        q_ref[:, sl] = (norm_rope(zq_ref[:, sl].astype(F32), 0) * SCALE).astype(q_ref.dtype)
    for br in range(3):
        for g in range(N_KV_HEADS):
            ck = br * 512 + g * HEAD_DIM
            cv = br * 512 + 256 + g * HEAD_DIM
            k = norm_rope(zkv_ref[:, ck:ck + HEAD_DIM].astype(F32), 1 + br)
            v = zkv_ref[:, cv:cv + HEAD_DIM].astype(F32)
            if br < 2:
                rows_ref[0, pl.ds(br * 4 + g, tr, stride=8), :] = k
                rows_ref[0, pl.ds(br * 4 + 2 + g, tr, stride=8), :] = v
            else:
                win_ref[0, pl.ds(g, tr, stride=4), :] = k
                win_ref[0, pl.ds(2 + g, tr, stride=4), :] = v
            if br == 0:
                cmpin_ref[g] = k
                cmpin_ref[2 + g] = v
            else:
                base = (br - 1) * 4
                kvatt_ref[base + g] = k.astype(kvatt_ref.dtype)
                kvatt_ref[base + 2 + g] = v.astype(kvatt_ref.dtype)
                if maybe_vt_ref:
                    for j in range(tr // 128):
                        maybe_vt_ref[0][(br - 1) * 2 + g, j] = v[j * 128:(j + 1) * 128].T.astype(kvatt_ref.dtype)


def _prep(z_main, cos2, sin2, gain8, tr, emit_vt, layer, depth, rows_all, win_all):
    m = z_main.shape[0]
    vt_specs = [pl.BlockSpec((4, tr // 128, HEAD_DIM, 128), lambda i: (0, i, 0, 0))] if emit_vt else []
    vt_shapes = [jax.ShapeDtypeStruct((4, m // 128, HEAD_DIM, 128), BF16)] if emit_vt else []
    return pl.pallas_call(
        functools.partial(_prep_kernel, tr=tr),
        grid=(m // tr,),
        in_specs=[pl.BlockSpec((tr, 1024), lambda i: (i, C_Q // 1024)),
                  pl.BlockSpec((tr, 1536), lambda i: (i, C_KV // 1536)),
                  pl.BlockSpec((tr, HEAD_DIM), lambda i: (i, 0)),
                  pl.BlockSpec((tr, HEAD_DIM), lambda i: (i, 0)),
                  pl.BlockSpec((8, HEAD_DIM), lambda i: (0, 0)),
                  pl.BlockSpec(memory_space=pl.ANY), pl.BlockSpec(memory_space=pl.ANY)],
        out_specs=[pl.BlockSpec((tr, 1024), lambda i: (i, 0)),
                   pl.BlockSpec((1, tr * 8, HEAD_DIM), lambda i: (layer, i, 0)),
                   pl.BlockSpec((1, tr * 4, HEAD_DIM), lambda i: (layer, i, 0)),
                   pl.BlockSpec((4, tr, HEAD_DIM), lambda i: (0, i, 0)),
                   pl.BlockSpec((8, tr, HEAD_DIM), lambda i: (0, i, 0))] + vt_specs,
        out_shape=[jax.ShapeDtypeStruct((m, 1024), BF16),
                   jax.ShapeDtypeStruct((depth, m * 8, HEAD_DIM), F32),
                   jax.ShapeDtypeStruct((depth, m * 4, HEAD_DIM), F32),
                   jax.ShapeDtypeStruct((4, m, HEAD_DIM), F32),
                   jax.ShapeDtypeStruct((8, m, HEAD_DIM), BF16)] + vt_shapes,
        input_output_aliases={5: 1, 6: 2},
        compiler_params=_cparams(("parallel",)),
        name="nsa_prep",
    )(z_main, z_main, cos2, sin2, gain8, rows_all, win_all)


def _compress_rows(load_pair, w_ref, c, n):
    acc = jnp.zeros((n, 256), F32)
    for j in range(8):
        xa, xb = load_pair(j)
        lhs = jnp.concatenate([xa, xb], axis=1).astype(BF16)
        acc = acc + jnp.dot(lhs, w_ref[c, j], preferred_element_type=F32)
    return acc


def _compress_const(pe_ref, w_ref, c):
    acc = jnp.zeros((16, 256), F32)
    for j in range(8):
        acc = acc + jnp.dot(pe_ref[c, j].astype(BF16), w_ref[c, j], preferred_element_type=F32)
    return acc[0:1, 0:128] + acc[1:2, 128:256]


def _compress_prompt_kernel(x_ref, w_ref, pe_ref, r_ref, c_ref, *, n):
    c = pl.program_id(1) // 2

    def load_pair(j):
        return (x_ref[0, 0, pl.ds(2 * j, n, stride=16), :],
                x_ref[0, 0, pl.ds(2 * j + 1, n, stride=16), :])

    r_ref[0, 0, 0:n, :] = _compress_rows(load_pair, w_ref, c, n)
    r_ref[0, 0, n:n + 8, :] = jnp.zeros((8, 256), F32)
    c_ref[0, 0] = jnp.broadcast_to(_compress_const(pe_ref, w_ref, c), (8, 128))


def _compress_prompt(cmpin, wcat, pe_lhs, bsz, t):
    n = t // CMP_STRIDE
    x = cmpin.reshape(4, bsz, t, HEAD_DIM)
    return pl.pallas_call(
        functools.partial(_compress_prompt_kernel, n=n),
        grid=(bsz, 4),
        in_specs=[pl.BlockSpec((1, 1, t, HEAD_DIM), lambda b, cg: (cg, b, 0, 0)),
                  pl.BlockSpec((2, 8, 256, 256), lambda b, cg: (0, 0, 0, 0)),
                  pl.BlockSpec((2, 8, 16, 256), lambda b, cg: (0, 0, 0, 0))],
        out_specs=[pl.BlockSpec((1, 1, n + 8, 256), lambda b, cg: (b, cg, 0, 0)),
                   pl.BlockSpec((1, 1, 8, 128), lambda b, cg: (b, cg, 0, 0))],
        out_shape=[jax.ShapeDtypeStruct((bsz, 4, n + 8, 256), F32),
                   jax.ShapeDtypeStruct((bsz, 4, 8, 128), F32)],
        compiler_params=_cparams(("parallel", "parallel")),
        name="compress_prompt",
    )(x, wcat, pe_lhs)


def _combine_cmp(r, cc, n, n_cmp):
    a = r[0:n, 0:128]
    bsh = r[1:n + 1, 128:256]
    ci = lax.broadcasted_iota(jnp.int32, (n, 1), 0)
    return jnp.where(ci < n_cmp, a + bsh + cc, 0.0)


def _softmax_rows(s, mask):
    s = jnp.where(mask, s, -jnp.inf)
    m = jnp.max(s, axis=-1, keepdims=True)
    m = jnp.where(m > -jnp.inf, m, 0.0)
    e = jnp.exp(s - m)
    return e / jnp.maximum(jnp.sum(e, axis=-1, keepdims=True), 1e-30)


def _cmp_and_select(q4, kc, vc, ov, t_q, n, n_cmp, n_slc, n_sel, nsp):
    nq = t_q.shape[0]
    t_rows = jnp.concatenate([t_q] * GQA_REP, axis=0)
    sc = _nt(q4, kc)
    ci = lax.broadcasted_iota(jnp.int32, (1, n), 1)
    cmask = (ci * CMP_STRIDE + (CMP_LEN - 1) <= t_rows) & (ci < n_cmp)
    pc = _softmax_rows(sc, cmask)
    o_cmp = jnp.dot(pc.astype(BF16), vc, preferred_element_type=F32)
    pcs = pc[0:nq] + pc[nq:2 * nq] + pc[2 * nq:3 * nq] + pc[3 * nq:4 * nq]
    hi = pcs.astype(BF16)
    lo = (pcs - hi.astype(F32)).astype(BF16)
    imp = jnp.dot(hi, ov, preferred_element_type=F32) + jnp.dot(lo, ov, preferred_element_type=F32)
    j = lax.broadcasted_iota(jnp.int32, (1, nsp), 1)
    cur = t_q >> SLC_SHIFT
    forced = (j == 0) | (j == cur) | (j == cur - 1)
    valid = (j * SLC_BLK <= t_q) & (j < n_slc)
    score = jnp.where(valid, imp + SEL_BONUS * forced.astype(F32), -jnp.inf)
    rank = jnp.zeros((nq, nsp), F32)
    for k in range(n_slc):
        col = score[:, k:k + 1]
        before = (col > score) | ((col == score) & (k < j))
        rank = rank + before.astype(F32)
    sel = (rank < float(n_sel)) & (j < n_slc)
    return o_cmp, sel.astype(BF16)


def _sel_mask(sel_win, win0, blk0, t_q, kpos0, nk):
    jb = win0 + lax.broadcasted_iota(jnp.int32, (128, nk), 0)
    kk = lax.broadcasted_iota(jnp.int32, (128, nk), 1)
    expand = (jb == blk0 + (kk >> SLC_SHIFT)).astype(BF16)
    hit = jnp.dot(sel_win, expand, preferred_element_type=F32)
    kpos = kpos0 + lax.broadcasted_iota(jnp.int32, (1, nk), 1)
    return (hit > 0.5) & (kpos <= t_q)


def _gate_and_write(o_ref_write, gts, nz, o_cmp, o_slc, o_win, nq):
    for r in range(GQA_REP):
        rs = slice(r * nq, (r + 1) * nq)
        o = (gts[:, 3 * r:3 * r + 1] * o_cmp[rs] + gts[:, 3 * r + 1:3 * r + 2] * o_slc[rs]
             + gts[:, 3 * r + 2:3 * r + 3] * o_win[rs])
        o_ref_write(r, o * _silu(nz[:, r * HEAD_DIM:(r + 1) * HEAD_DIM]))


def _nsa_prompt_kernel(q_ref, rk_ref, rv_ref, ck_ref, cv_ref, ks_ref, vst_ref, kw_ref, vwt_ref, ovt_ref,
                       gz_ref, nz_ref, o_ref, kc_scr, vct_scr, *, n, n_cmp, n_slc, n_sel, nsb, tk, nwt):
    i = pl.program_id(2)
    nq = Q_BLOCK
    nr = GQA_REP * nq
    qb = q_ref[0]
    q4t = jnp.concatenate([qb[:, r * HEAD_DIM:(r + 1) * HEAD_DIM].astype(F32).T.astype(BF16)
                           for r in range(GQA_REP)], axis=1)
    t_q = i * nq + lax.broadcasted_iota(jnp.int32, (1, nq), 1)
    t_q4 = jnp.concatenate([t_q] * GQA_REP, axis=1)

    @pl.when(i == 0)
    def _():
        kc_scr[...] = _combine_cmp(rk_ref.at[0, 0], ck_ref[0, 0, 0:1, :], n, n_cmp).astype(BF16)
        vct_scr[...] = _combine_cmp(rv_ref.at[0, 0], cv_ref[0, 0, 0:1, :], n, n_cmp).T.astype(BF16)

    kc = kc_scr[...]
    vct = vct_scr[...]
    sc = jnp.dot(kc, q4t, preferred_element_type=F32)
    ci = lax.broadcasted_iota(jnp.int32, (n, 1), 0)
    cmask = (ci * CMP_STRIDE + (CMP_LEN - 1) <= t_q4) & (ci < n_cmp)
    sc = jnp.where(cmask, sc, -jnp.inf)
    mx = jnp.max(sc, axis=0, keepdims=True)
    mx = jnp.where(mx > -jnp.inf, mx, 0.0)
    e = jnp.exp(sc - mx)
    pc = e * (1.0 / jnp.maximum(jnp.sum(e, axis=0, keepdims=True), 1e-30))
    o_cmp = jnp.dot(vct, pc.astype(BF16), preferred_element_type=F32)
    pcs = pc[:, 0:nq] + pc[:, nq:2 * nq] + pc[:, 2 * nq:3 * nq] + pc[:, 3 * nq:4 * nq]
    hi = pcs.astype(BF16)
    lo = (pcs - hi.astype(F32)).astype(BF16)
    ovt = ovt_ref[...]
    imp = jnp.dot(ovt, hi, preferred_element_type=F32) + jnp.dot(ovt, lo, preferred_element_type=F32)

    jb = lax.broadcasted_iota(jnp.int32, (nsb, 1), 0)
    cur = t_q >> SLC_SHIFT
    forced = (jb == 0) | (jb == cur) | (jb == cur - 1)
    valid = (jb * SLC_BLK <= t_q) & (jb < n_slc)
    score = jnp.where(valid, imp + SEL_BONUS * forced.astype(F32), -jnp.inf)
    rank = jnp.zeros((nsb, nq), F32)
    for k in range(n_slc):
        row = score[k:k + 1, :]
        before = (row > score) | ((row == score) & (k < jb))
        rank = rank + before.astype(F32)
    sel = ((rank < float(n_sel)) & (jb < n_slc)).astype(BF16)
    if nsb < 128:
        sel = jnp.concatenate([sel, jnp.zeros((128 - nsb, nq), BF16)], axis=0)

    def masked_scores(k, mq):
        st = jnp.dot(k, q4t, preferred_element_type=F32)
        return jnp.concatenate([jnp.where(mq, st[:, r * nq:(r + 1) * nq], -jnp.inf)
                                for r in range(GQA_REP)], axis=1)

    def keys_at(k_ref, vt_ref, tile0, ntiles):
        k = k_ref[0, 0, pl.ds(pl.multiple_of(tile0 * 128, 128), ntiles * 128), :]
        vt = jnp.concatenate([vt_ref[0, 0, tile0 + j] for j in range(ntiles)], axis=1)
        kpos = tile0 * 128 + lax.broadcasted_iota(jnp.int32, (ntiles * 128, 1), 0)
        return k, vt, kpos

    tpt = tk // 128

    def slc_body(kb, carry):
        m, l, acc = carry
        k, vt, kpos = keys_at(ks_ref, vst_ref, kb * tpt, tpt)
        jl = lax.broadcasted_iota(jnp.int32, (tk, 128), 1)
        expand = (jl == (kpos >> SLC_SHIFT)).astype(BF16)
        hit = jnp.dot(expand, sel, preferred_element_type=F32)
        st = masked_scores(k, (hit > 0.5) & (kpos <= t_q))
        m_new = jnp.maximum(m, jnp.max(st, axis=0, keepdims=True))
        p = jnp.exp(st - m_new)
        alpha = jnp.exp(m - m_new)
        l_new = alpha * l + jnp.sum(p, axis=0, keepdims=True)
        acc_new = alpha * acc + jnp.dot(vt, p.astype(BF16), preferred_element_type=F32)
        return m_new, l_new, acc_new

    init = (jnp.full((1, nr), NEG_BIG, F32), jnp.zeros((1, nr), F32), jnp.zeros((HEAD_DIM, nr), F32))
    m, l, acc = lax.fori_loop(0, (i * nq + nq - 1) // tk + 1, slc_body, init)
    o_slc = acc * (1.0 / jnp.maximum(l, 1e-30))

    k, vt, kpos = keys_at(kw_ref, vwt_ref, jnp.maximum(i + 1 - nwt, 0), nwt)
    d = t_q - kpos
    st = masked_scores(k, (d >= 0) & (d < WINDOW))
    mw = jnp.max(st, axis=0, keepdims=True)
    mw = jnp.where(mw > -jnp.inf, mw, 0.0)
    p = jnp.exp(st - mw)
    o_win = (jnp.dot(vt, p.astype(BF16), preferred_element_type=F32)
             * (1.0 / jnp.maximum(jnp.sum(p, axis=0, keepdims=True), 1e-30)))

    gt = jax.nn.sigmoid(gz_ref[0]).T
    nz = nz_ref[0].astype(F32)
    for r in range(GQA_REP):
        cs = slice(r * nq, (r + 1) * nq)
        ot = (gt[3 * r:3 * r + 1, :] * o_cmp[:, cs] + gt[3 * r + 1:3 * r + 2, :] * o_slc[:, cs]
              + gt[3 * r + 2:3 * r + 3, :] * o_win[:, cs])
        hs = slice(r * HEAD_DIM, (r + 1) * HEAD_DIM)
        o_ref[0, :, hs] = (ot.T * _silu(nz[:, hs])).astype(o_ref.dtype)


def _nsa_prompt(q_r, r_cmp, c_cmp, kv_att, v_t, ovt, z_small, z_main, bsz, t, tk):
    n = t // CMP_STRIDE
    n_cmp = n - CMP_LEN // CMP_STRIDE + 1
    n_slc = -(-t // SLC_BLK)
    nsb = ovt.shape[0]
    q3 = q_r.reshape(bsz, t, 1024)
    kv4 = kv_att.reshape(8, bsz, t, HEAD_DIM)
    vt5 = v_t.reshape(4, bsz, t // 128, HEAD_DIM, 128)
    nwt = min(WINDOW // Q_BLOCK + 1, t // 128)
    zs3 = z_small.reshape(bsz, t, N_SMALL)
    zm3 = z_main.reshape(bsz, t, N_MAIN)
    rspec = lambda off: pl.BlockSpec((1, 1, n + 8, 256), lambda b, g, i: (b, off + g, 0, 0))
    cspec = lambda off: pl.BlockSpec((1, 1, 8, 128), lambda b, g, i: (b, off + g, 0, 0))
    kspec = lambda off: pl.BlockSpec((1, 1, t, HEAD_DIM), lambda b, g, i: (off + g, b, 0, 0))
    vtspec = lambda off: pl.BlockSpec((1, 1, t // 128, HEAD_DIM, 128), lambda b, g, i: (off + g, b, 0, 0, 0))
    out = pl.pallas_call(
        functools.partial(_nsa_prompt_kernel, n=n, n_cmp=n_cmp, n_slc=n_slc, n_sel=min(N_SEL, n_slc), nsb=nsb,
                          tk=tk, nwt=nwt),
        grid=(bsz, N_KV_HEADS, t // Q_BLOCK),
        in_specs=[pl.BlockSpec((1, Q_BLOCK, 512), lambda b, g, i: (b, i, g)),
                  rspec(0), rspec(2), cspec(0), cspec(2),
                  kspec(0), vtspec(0), kspec(4), vtspec(2),
                  pl.BlockSpec(ovt.shape, lambda b, g, i: (0, 0)),
                  pl.BlockSpec((1, Q_BLOCK, 128), lambda b, g, i: (b, i, 1 + g)),
                  pl.BlockSpec((1, Q_BLOCK, 512), lambda b, g, i: (b, i, C_NSAZ // 512 + g))],
        out_specs=pl.BlockSpec((1, Q_BLOCK, 512), lambda b, g, i: (b, i, g)),
        out_shape=jax.ShapeDtypeStruct((bsz, t, 1024), BF16),
        scratch_shapes=[pltpu.VMEM((n, HEAD_DIM), BF16), pltpu.VMEM((HEAD_DIM, n), BF16)],
        compiler_params=_cparams(("parallel", "parallel", "arbitrary")),
        name="nsa_prompt",
    )(q3, r_cmp, r_cmp, c_cmp, c_cmp, kv4, vt5, kv4, vt5, ovt, zs3, zm3)
    return out.reshape(bsz * t, 1024)


def _nsa_decode_kernel(pt_ref, cache_ref, q_ref, w_ref, pe_ref, new_ref, cw_ref, ov_ref, zs_ref, nz_ref, o_ref,
                       buf, sem, r_scr, sc_scr, v_scr,
                       *, layer, pp, ns, n, n_cmp, n_slc, n_sel, nsp, past_len, nq):
    b = pl.program_id(0)
    s = pl.program_id(1)
    step = b * ns + s
    nsteps = pl.num_programs(0) * ns
    slot = step % 2
    nch = pp * 8
    nk = pp * 128
    nr = GQA_REP * nq

    def copies(bb, ss, sl):
        return [pltpu.make_async_copy(cache_ref.at[layer, pt_ref[bb, ss * pp + j]],
                                      buf.at[sl, pl.ds(j * 1024, 1024), :], sem.at[sl]) for j in range(pp)]

    @pl.when(step == 0)
    def _():
        for cp in copies(b, s, slot):
            cp.start()

    @pl.when(step + 1 < nsteps)
    def _():
        nxt = step + 1
        for cp in copies(nxt // ns, nxt % ns, 1 - slot):
            cp.start()

    for cp in copies(b, s, slot):
        cp.wait()

    qb = q_ref[0]
    q4s = [jnp.concatenate([qb[:, (g * GQA_REP + r) * HEAD_DIM:(g * GQA_REP + r + 1) * HEAD_DIM]
                            for r in range(GQA_REP)], axis=0) for g in range(N_KV_HEADS)]

    @pl.when(s == 0)
    def _():
        r_scr[:, ns * nch:ns * nch + 8, :] = jnp.zeros((4, 8, 256), F32)

    row0 = pl.multiple_of(s * nch, nch)
    for c in range(2):
        acc = jnp.zeros((N_KV_HEADS * nch, 256), F32)
        for j in range(8):
            def chunk_rows(l, g):
                return buf[slot, pl.ds(l * 8 + c * 2 + g, nch, stride=128), :]
            lhs = jnp.concatenate(
                [jnp.concatenate([chunk_rows(2 * j, g), chunk_rows(2 * j + 1, g)], axis=1)
                 for g in range(N_KV_HEADS)], axis=0).astype(BF16)
            acc = acc + jnp.dot(lhs, w_ref[c, j], preferred_element_type=F32)
        for g in range(N_KV_HEADS):
            r_scr[c * 2 + g, pl.ds(row0, nch), :] = acc[g * nch:(g + 1) * nch]

    key0 = pl.multiple_of(s * nk, nk)
    for g in range(N_KV_HEADS):
        k = buf[slot, pl.ds(4 + g, nk, stride=8), :].astype(BF16)
        sc_scr[g, s] = _nt(q4s[g], k)
        v_scr[g, pl.ds(key0, nk), :] = buf[slot, pl.ds(6 + g, nk, stride=8), :].astype(BF16)

    @pl.when(s == ns - 1)
    def _():
        t_q = past_len + lax.broadcasted_iota(jnp.int32, (nq, 1), 0)
        t_rows = jnp.concatenate([t_q] * GQA_REP, axis=0)
        gts_all = jax.nn.sigmoid(zs_ref[0])
        nz_all = nz_ref[0].astype(F32)
        cck = _compress_const(pe_ref, w_ref, 0)
        ccv = _compress_const(pe_ref, w_ref, 1)
        for g in range(N_KV_HEADS):
            q4 = q4s[g]
            kc = _combine_cmp(r_scr.at[g], cck, n, n_cmp).astype(BF16)
            vc = _combine_cmp(r_scr.at[2 + g], ccv, n, n_cmp).astype(BF16)
            o_cmp, sel = _cmp_and_select(q4, kc, vc, ov_ref[...], t_q, n, n_cmp, n_slc, n_sel, nsp)

            def chunk_mask(blk0, kpos0, nkeys):
                win0 = (blk0 // 128) * 128
                mq = _sel_mask(sel[:, win0:win0 + 128], win0, blk0, t_q, kpos0, nkeys)
                return jnp.concatenate([mq] * GQA_REP, axis=0)

            mx = jnp.full((nr, 1), -jnp.inf, F32)
            for cidx in range(ns):
                sm = jnp.where(chunk_mask(cidx * pp * 2, cidx * nk, nk), sc_scr[g, cidx], -jnp.inf)
                sc_scr[g, cidx] = sm
                mx = jnp.maximum(mx, jnp.max(sm, axis=-1, keepdims=True))
            s_new = jnp.where(chunk_mask(past_len // SLC_BLK, past_len, 128),
                              _nt(q4, new_ref[g, 0]), -jnp.inf)
            mx = jnp.maximum(mx, jnp.max(s_new, axis=-1, keepdims=True))
            mx = jnp.where(mx > -jnp.inf, mx, 0.0)
            p_new = jnp.exp(s_new - mx)
            den = jnp.sum(p_new, axis=-1, keepdims=True)
            acc = jnp.dot(p_new.astype(BF16), new_ref[2 + g, 0], preferred_element_type=F32)
            for cidx in range(ns):
                p = jnp.exp(sc_scr[g, cidx] - mx)
                den = den + jnp.sum(p, axis=-1, keepdims=True)
                acc = acc + jnp.dot(p.astype(BF16), v_scr[g, cidx * nk:(cidx + 1) * nk, :],
                                    preferred_element_type=F32)
            o_slc = acc / jnp.maximum(den, 1e-30)

            kw = cw_ref[0, 0, pl.ds(g, WINDOW, stride=4), :].astype(BF16)
            vw = cw_ref[0, 0, pl.ds(2 + g, WINDOW, stride=4), :].astype(BF16)
            d1 = t_rows - (past_len - WINDOW + lax.broadcasted_iota(jnp.int32, (1, WINDOW), 1))
            s1 = jnp.where((d1 >= 0) & (d1 < WINDOW), _nt(q4, kw), -jnp.inf)
            d2 = t_rows - (past_len + lax.broadcasted_iota(jnp.int32, (1, 128), 1))
            s2 = jnp.where((d2 >= 0) & (d2 < WINDOW), _nt(q4, new_ref[4 + g, 0]), -jnp.inf)
            mw = jnp.maximum(jnp.max(s1, axis=-1, keepdims=True), jnp.max(s2, axis=-1, keepdims=True))
            mw = jnp.where(mw > -jnp.inf, mw, 0.0)
            p1 = jnp.exp(s1 - mw)
            p2 = jnp.exp(s2 - mw)
            denw = jnp.sum(p1, axis=-1, keepdims=True) + jnp.sum(p2, axis=-1, keepdims=True)
            o_win = (jnp.dot(p1.astype(BF16), vw, preferred_element_type=F32)
                     + jnp.dot(p2.astype(BF16), new_ref[6 + g, 0], preferred_element_type=F32))
            o_win = o_win / jnp.maximum(denw, 1e-30)

            def write(r, val, g=g):
                c0 = (g * GQA_REP + r) * HEAD_DIM
                o_ref[0, :, c0:c0 + HEAD_DIM] = val.astype(o_ref.dtype)

            _gate_and_write(write, gts_all[:, 128 * (1 + g):128 * (2 + g)],
                            nz_all[:, g * 512:(g + 1) * 512], o_cmp, o_slc, o_win, nq)


def _nsa_decode(page_table, q_s, cache_pages, wcat, pe_lhs, new_att, cache_win_rows, ov, z_small, z_main, layer,
                past_len, nq, dec_t, pp):
    dec_b, n_pages = page_table.shape
    ns = n_pages // pp
    n = n_pages * 8
    tk = past_len + dec_t
    n_cmp = tk // CMP_STRIDE - CMP_LEN // CMP_STRIDE + 1
    n_slc = -(-tk // SLC_BLK)
    nsp = ov.shape[1]
    nr = GQA_REP * nq
    q3 = q_s.reshape(dec_b, nq, 1024)
    zs3 = z_small.reshape(dec_b, nq, N_SMALL)
    zm3 = z_main.reshape(dec_b, nq, N_MAIN)
    grid_spec = pltpu.PrefetchScalarGridSpec(
        num_scalar_prefetch=1,
        grid=(dec_b, ns),
        in_specs=[pl.BlockSpec(memory_space=pl.ANY),
                  pl.BlockSpec((1, nq, 1024), lambda b, s, pt: (b, 0, 0)),
                  pl.BlockSpec((2, 8, 256, 256), lambda b, s, pt: (0, 0, 0, 0)),
                  pl.BlockSpec((2, 8, 16, 256), lambda b, s, pt: (0, 0, 0, 0)),
                  pl.BlockSpec((8, 1, 128, HEAD_DIM), lambda b, s, pt: (0, b, 0, 0)),
                  pl.BlockSpec((1, 1, 4 * WINDOW, HEAD_DIM), lambda b, s, pt: (layer, b, 0, 0)),
                  pl.BlockSpec(ov.shape, lambda b, s, pt: (0, 0)),
                  pl.BlockSpec((1, nq, N_SMALL), lambda b, s, pt: (b, 0, 0)),
                  pl.BlockSpec((1, nq, 1024), lambda b, s, pt: (b, 0, C_NSAZ // 1024))],
        out_specs=pl.BlockSpec((1, nq, 1024), lambda b, s, pt: (b, 0, 0)),
        scratch_shapes=[pltpu.VMEM((2, pp * 1024, HEAD_DIM), F32),
                        pltpu.SemaphoreType.DMA((2,)),
                        pltpu.VMEM((4, n + 8, 256), F32),
                        pltpu.VMEM((N_KV_HEADS, ns, nr, pp * 128), F32),
                        pltpu.VMEM((N_KV_HEADS, n_pages * 128, HEAD_DIM), BF16)],
    )
    out = pl.pallas_call(
        functools.partial(_nsa_decode_kernel, layer=layer, pp=pp, ns=ns, n=n, n_cmp=n_cmp, n_slc=n_slc,
                          n_sel=min(N_SEL, n_slc), nsp=nsp, past_len=past_len, nq=nq),
        grid_spec=grid_spec,
        out_shape=jax.ShapeDtypeStruct((dec_b, nq, 1024), BF16),
        compiler_params=_cparams(("arbitrary", "arbitrary")),
        name="nsa_decode",
    )(page_table, cache_pages, q3, wcat, pe_lhs, new_att, cache_win_rows, ov, zs3, zm3)
    return out.reshape(dec_b * nq, 1024)


def _pool_kernel(u_ref, pz_ref, pre_ref, w_ref, sc_ref, o_ref, buf, *, tr, pos0):
    i = pl.program_id(1)

    @pl.when(i == 0)
    def _():
        buf[0:16, :] = pre_ref[0]

    u = u_ref[0].astype(F32)
    buf[16:16 + tr, :] = u
    pos = pos0 + i * tr + lax.broadcasted_iota(jnp.int32, (tr, 1), 0)
    for gi, w in enumerate(POOL_WINDOWS):
        cs = slice(gi * POOL_GROUP, (gi + 1) * POOL_GROUP)
        s = buf[16:16 + tr, cs]
        for k in range(1, w):
            s = s + buf[16 - k:16 - k + tr, cs]
        cnt = jnp.minimum(w, pos + 1).astype(F32)
        pooled = s / cnt - u[:, cs]
        mixed = jnp.dot(pooled.astype(BF16), w_ref[gi], preferred_element_type=F32) * sc_ref[:, cs]
        o_ref[0, :, cs] = (mixed * _silu(pz_ref[0, :, cs].astype(F32))).astype(o_ref.dtype)
    buf[0:16, :] = buf[tr:tr + 16, :]


def _pool(z_main, prefix16, pool_w, pool_scale, bsz, t, tr, pos0):
    zm3 = z_main.reshape(bsz, t, N_MAIN)
    out = pl.pallas_call(
        functools.partial(_pool_kernel, tr=tr, pos0=pos0),
        grid=(bsz, t // tr),
        in_specs=[pl.BlockSpec((1, tr, 1024), lambda b, i: (b, i, C_POOLU // 1024)),
                  pl.BlockSpec((1, tr, 1024), lambda b, i: (b, i, C_POOLZ // 1024)),
                  pl.BlockSpec((1, 16, 1024), lambda b, i: (b, 0, 0)),
                  pl.BlockSpec((4, POOL_GROUP, POOL_GROUP), lambda b, i: (0, 0, 0)),
                  pl.BlockSpec((1, 1024), lambda b, i: (0, 0))],
        out_specs=pl.BlockSpec((1, tr, 1024), lambda b, i: (b, i, 0)),
        out_shape=jax.ShapeDtypeStruct((bsz, t, 1024), BF16),
        scratch_shapes=[pltpu.VMEM((16 + tr, 1024), F32)],
        compiler_params=_cparams(("arbitrary", "arbitrary")),
        name="pool",
    )(zm3, zm3, prefix16, pool_w, pool_scale.reshape(1, 1024))
    return out.reshape(bsz * t, 1024)


def _ssd_kernel(xs_ref, bm_ref, cm_ref, dt_ref, mz_ref, pre_ref, init_ref, cw_ref, cb_ref, dtb_ref,
                alp_ref, dsk_ref, mn_ref, ex_ref, y_ref, fin_ref, xbuf, st, *, rows_in, valid_len):
    c = pl.program_id(1)
    nc = pl.num_programs(1)
    L = SSD_CHUNK

    @pl.when(c == 0)
    def _():
        xbuf[0:8, :] = pre_ref[0]
        for pr in range(8):
            st[:, pr * 128:(pr + 1) * 128] = init_ref[0, pr * 128:(pr + 1) * 128, :].T

    if rows_in < L:
        xbuf[8 + rows_in:8 + L, :] = jnp.zeros((L - rows_in, CONV_CH), F32)
    xbuf[8:8 + rows_in, 0:1024] = xs_ref[0].astype(F32)
    xbuf[8:8 + rows_in, 1024:1280] = bm_ref[0].astype(F32)
    xbuf[8:8 + rows_in, 1280:1536] = cm_ref[0].astype(F32)
    acc = cb_ref[...] + xbuf[5:5 + L, :] * cw_ref[0:1, :]
    for j in range(1, CONV_W):
        acc = acc + xbuf[5 + j:5 + j + L, :] * cw_ref[j:j + 1, :]
    xbuf[0:8, :] = xbuf[L:L + 8, :]
    xc = _silu(acc)
    xs = xc[:, 0:1024]
    bmat = xc[:, 1024:1280]
    cmat = xc[:, 1280:1536]

    if rows_in < L:
        dtz = jnp.concatenate([dt_ref[0], jnp.zeros((L - rows_in, 128), F32)], axis=0)
    else:
        dtz = dt_ref[0]
    dtz = dtz + dtb_ref[...]
    dt = jnp.maximum(dtz, 0.0) + jnp.log1p(jnp.exp(-jnp.abs(dtz)))
    row = c * L + lax.broadcasted_iota(jnp.int32, (L, 1), 0)
    dt = jnp.where(row < valid_len, dt, 0.0)

    li = lax.broadcasted_iota(jnp.int32, (L, L), 0)
    si = lax.broadcasted_iota(jnp.int32, (L, L), 1)
    tril_b = li >= si
    tril = tril_b.astype(F32)
    def per_channel(v):
        hi = v.astype(BF16)
        r1 = v - hi.astype(F32)
        mid = r1.astype(BF16)
        lo = (r1 - mid.astype(F32)).astype(BF16)
        e = ex_ref[...]
        return (jnp.dot(hi, e, preferred_element_type=F32) + jnp.dot(mid, e, preferred_element_type=F32)
                + jnp.dot(lo, e, preferred_element_type=F32))

    dt_x = per_channel(dt)
    la_p = dt * (-jnp.exp(alp_ref[...]))
    acs_p = jnp.dot(tril, la_p, precision=HI, preferred_element_type=F32)
    acs_pt = acs_p.T
    acs_x = per_channel(acs_p)

    xdt = xs * dt_x
    a_end = acs_x[L - 1:L, :]
    eacs = jnp.exp(acs_x)
    xw = (xdt * jnp.exp(a_end - acs_x)).astype(BF16)
    xdt_b = xdt.astype(BF16)
    lane = lax.broadcasted_iota(jnp.int32, (1, 128), 1)
    y_parts = []
    for g in range(M_GROUPS):
        bg = bmat[:, g * M_STATE:(g + 1) * M_STATE]
        cg = cmat[:, g * M_STATE:(g + 1) * M_STATE].astype(BF16)
        bgt = bg.T.astype(BF16)
        gmat = jnp.dot(cg, bgt, preferred_element_type=F32)
        for hp in range(M_HEADS // M_GROUPS // 2):
            pr = g * (M_HEADS // M_GROUPS // 2) + hp
            cs = slice(pr * 128, (pr + 1) * 128)
            xpair = xdt_b[:, cs]
            ydiag = jnp.zeros((L, 128), F32)
            for half in range(2):
                h = 2 * pr + half
                col = acs_x[:, h * M_HEAD_DIM:h * M_HEAD_DIM + 1]
                rw = acs_pt[h:h + 1, :]
                dec = jnp.where(tril_b, jnp.exp(col - rw), 0.0)
                mm = (gmat * dec).astype(BF16)
                in_half = (lane >= half * M_HEAD_DIM) & (lane < (half + 1) * M_HEAD_DIM)
                rhs = jnp.where(in_half, xpair, jnp.zeros_like(xpair))
                ydiag = ydiag + jnp.dot(mm, rhs, preferred_element_type=F32)
            prev = st[:, cs]
            yoff = jnp.dot(cg, prev.astype(BF16), preferred_element_type=F32) * eacs[:, cs]
            st[:, cs] = jnp.exp(a_end[:, cs]) * prev + jnp.dot(bgt, xw[:, cs], preferred_element_type=F32)
            y_parts.append(ydiag + yoff)
    y = jnp.concatenate(y_parts, axis=1)
    y = y + dsk_ref[...] * xs
    if rows_in < L:
        mz = jnp.concatenate([mz_ref[0].astype(F32), jnp.zeros((L - rows_in, 1024), F32)], axis=0)
    else:
        mz = mz_ref[0].astype(F32)
    y = y * _silu(mz)
    half_w = BRANCH_WIDTH // M_GROUPS
    outs = []
    for g in range(M_GROUPS):
        yg = y[:, g * half_w:(g + 1) * half_w]
        outs.append(yg * lax.rsqrt(jnp.mean(yg * yg, axis=-1, keepdims=True) + EPS))
    yn = jnp.concatenate(outs, axis=1) * mn_ref[...]
    y_ref[0] = yn[0:rows_in].astype(y_ref.dtype)

    @pl.when(c == nc - 1)
    def _():
        for pr in range(8):
            fin_ref[0, pr * 128:(pr + 1) * 128, :] = st[:, pr * 128:(pr + 1) * 128].T


def _ssd(z_main, z_small, prefix8, init, consts, bsz, t, rows_in, valid_len):
    conv_w8, conv_b, dtb, alp, dskx, mnorm, expand = consts
    zm3 = z_main.reshape(bsz, t, N_MAIN)
    zs3 = z_small.reshape(bsz, t, N_SMALL)
    nc = t // rows_in
    full = lambda shp: pl.BlockSpec(shp, lambda b, c: tuple(0 for _ in shp))
    y, fin = pl.pallas_call(
        functools.partial(_ssd_kernel, rows_in=rows_in, valid_len=valid_len),
        grid=(bsz, nc),
        in_specs=[pl.BlockSpec((1, rows_in, 1024), lambda b, c: (b, c, C_XS // 1024)),
                  pl.BlockSpec((1, rows_in, 256), lambda b, c: (b, c, C_B // 256)),
                  pl.BlockSpec((1, rows_in, 256), lambda b, c: (b, c, C_C // 256)),
                  pl.BlockSpec((1, rows_in, 128), lambda b, c: (b, c, 0)),
                  pl.BlockSpec((1, rows_in, 1024), lambda b, c: (b, c, C_MZ // 1024)),
                  pl.BlockSpec((1, 8, CONV_CH), lambda b, c: (b, 0, 0)),
                  pl.BlockSpec((1, 1024, M_STATE), lambda b, c: (b, 0, 0)),
                  full((8, CONV_CH)), full((1, CONV_CH)), full((1, 128)), full((1, 128)),
                  full((1, 1024)), full((1, 1024)), full((128, 1024))],
        out_specs=[pl.BlockSpec((1, rows_in, 1024), lambda b, c: (b, c, 0)),
                   pl.BlockSpec((1, 1024, M_STATE), lambda b, c: (b, 0, 0))],
        out_shape=[jax.ShapeDtypeStruct((bsz, t, 1024), BF16),
                   jax.ShapeDtypeStruct((bsz, 1024, M_STATE), F32)],
        scratch_shapes=[pltpu.VMEM((8 + SSD_CHUNK, CONV_CH), F32), pltpu.VMEM((M_STATE, 1024), F32)],
        compiler_params=_cparams(("arbitrary", "arbitrary")),
        name="ssd",
    )(zm3, zm3, zm3, zs3, zm3, prefix8, init, conv_w8, conv_b, dtb, alp, dskx, mnorm, expand)
    return y.reshape(bsz * t, 1024), fin


def _overlap_matrix(n, n_cmp, n_slc, nsp):
    i = np.arange(n)[:, None]
    j = np.arange(nsp)[None, :]
    ov = (i * CMP_STRIDE < j * SLC_BLK + SLC_BLK) & (i * CMP_STRIDE + CMP_LEN > j * SLC_BLK)
    ov = ov & (i < n_cmp) & (j < n_slc)
    return jnp.asarray(ov.astype(np.float32), dtype=BF16)


def _rope_tables(pos):
    inv = ROPE_THETA ** (-jnp.arange(0, HEAD_DIM, 2, dtype=F32) / HEAD_DIM)
    ang = pos.astype(F32)[:, None] * inv[None, :]
    cos, sin = jnp.cos(ang), jnp.sin(ang)
    return jnp.concatenate([cos, cos], axis=1), jnp.concatenate([-sin, sin], axis=1)


def _layer_consts(l, w_in, qk_gain, cmp_pe, cmp_w, pool_w, conv_w, conv_b, dt_bias, a_log, d_skip, mnorm_w,
                  w_branch, w_out):
    offs = np.cumsum((0, 1024, 1536, 24, 1024, 1024, 1024, 1024, CONV_CH, M_HEADS, 3 * D_MODEL))
    seg = lambda k: w_in[l][:, offs[k]:offs[k + 1]]
    xbc = seg(7)
    w_main = jnp.concatenate([seg(0), seg(3), seg(4), seg(5), seg(6), xbc[:, :1024], seg(9), seg(1),
                              xbc[:, 1024:1280], xbc[:, 1280:1536]], axis=1).astype(BF16)
    gates = seg(2)
    zpad = lambda w: jnp.zeros((D_MODEL, w), F32)
    w_small = jnp.concatenate([seg(8), zpad(128 - M_HEADS), gates[:, :12], zpad(116), gates[:, 12:], zpad(116)],
                              axis=1).astype(BF16)
    gain8 = jnp.concatenate([qk_gain[l], jnp.zeros((4, HEAD_DIM), F32)], axis=0)
    w4 = cmp_w[l].reshape(2, 2, 8, 2, HEAD_DIM, HEAD_DIM)
    wcat = jnp.transpose(w4, (0, 2, 3, 4, 1, 5)).reshape(2, 8, 256, 256).astype(BF16)
    pe4 = cmp_pe[l].reshape(2, 2, 8, 2 * HEAD_DIM)
    pe_lhs = jnp.concatenate([jnp.transpose(pe4, (0, 2, 1, 3)), jnp.zeros((2, 8, 14, 256), F32)], axis=2)
    rep = lambda v: jnp.repeat(v, M_HEAD_DIM).reshape(1, 1024)
    pad128 = lambda v: jnp.concatenate([v, jnp.zeros((128 - v.shape[0],), F32)]).reshape(1, 128)
    expand = (np.arange(128)[:, None] == (np.arange(1024)[None, :] // M_HEAD_DIM)).astype(np.float32)
    ssd_consts = (jnp.concatenate([conv_w[l], jnp.zeros((4, CONV_CH), F32)], axis=0), conv_b[l].reshape(1, CONV_CH),
                  pad128(dt_bias[l]), pad128(a_log[l]), rep(d_skip[l]), mnorm_w[l].reshape(1, 1024),
                  jnp.asarray(expand, dtype=BF16))
    return dict(w_main=w_main, w_small=w_small, gain8=gain8, wcat=wcat, pe_lhs=pe_lhs,
                pool_w=pool_w[l].astype(BF16), ssd=ssd_consts, w_branch=w_branch[l].astype(BF16),
                w_out=w_out[l].astype(BF16))


def _xbc_cols(z3):
    return jnp.concatenate([z3[..., C_XS:C_XS + 1024], z3[..., C_B:C_B + 256], z3[..., C_C:C_C + 256]], axis=-1)


def _tile(m, pref):
    for t in pref:
        if m % t == 0:
            return t
    return m


def kernel(x_prompt, x_sample, cache_kv, cache_win, state_pool, state_conv, state_ssm, page_table, norm_w, w_in,
           qk_gain, cmp_pe, cmp_w, pool_w, pool_scale, conv_w, conv_b, dt_bias, a_log, d_skip, mnorm_w, w_branch,
           w_out):
    depth = w_in.shape[0]
    bsz, seq, _ = x_prompt.shape
    dec_b, dec_t, _ = x_sample.shape
    n_pool, page = cache_kv.shape[1], cache_kv.shape[2]
    n_pages = page_table.shape[1]
    past_len = n_pages * page
    assert page == 128 and dec_t == 4 and cache_win.shape[2] == WINDOW and seq % 128 == 0
    nq = 8
    cache_pages = cache_kv.reshape(depth, n_pool, page * 4 * N_KV_HEADS, HEAD_DIM)
    cache_win_rows = cache_win.reshape(depth, dec_b, WINDOW * 2 * N_KV_HEADS, HEAD_DIM)
    tk_p = _tile(seq, (1024, 512, 256, 128))

    mp = bsz * seq
    ms = dec_b * nq
    xp = x_prompt.reshape(mp, D_MODEL)
    xs = jnp.pad(x_sample, ((0, 0), (0, nq - dec_t), (0, 0))).reshape(ms, D_MODEL)

    cos_p, sin_p = _rope_tables(jnp.tile(jnp.arange(seq), bsz))
    cos_s, sin_s = _rope_tables(jnp.tile(past_len + jnp.arange(nq), dec_b))

    n_p = seq // CMP_STRIDE
    n_slc_p = -(-seq // SLC_BLK)
    nsb_p = -(-n_slc_p // 16) * 16
    assert nsb_p <= 128
    ovt_p = _overlap_matrix(n_p, n_p - 1, n_slc_p, nsb_p).T
    n_s = n_pages * 8
    tk = past_len + dec_t
    n_slc_s = -(-tk // SLC_BLK)
    nsp_s = -(-n_slc_s // 128) * 128
    ov_s = _overlap_matrix(n_s, tk // CMP_STRIDE - 1, n_slc_s, nsp_s)

    tm_p = _tile(mp, (512, 256, 128))
    tm_big = _tile(mp, (1024, 512, 256, 128))
    outs_p = [[] for _ in range(3)]
    outs_s = [[] for _ in range(3)]
    rows_p = jnp.zeros((depth, mp * 8, HEAD_DIM), F32)
    win_p = jnp.zeros((depth, mp * 4, HEAD_DIM), F32)
    rows_s = jnp.zeros((depth, ms * 8, HEAD_DIM), F32)
    win_s = jnp.zeros((depth, ms * 4, HEAD_DIM), F32)
    hp = _rmsnorm(xp, norm_w[0], tm_p)
    hs = _rmsnorm(xs, norm_w[0], ms)
    for l in range(depth):
        c = _layer_consts(l, w_in, qk_gain, cmp_pe, cmp_w, pool_w, conv_w, conv_b, dt_bias, a_log, d_skip,
                          mnorm_w, w_branch, w_out)
        g_next = norm_w[min(l + 1, depth - 1)]
        z_main = _matmul(hp, c["w_main"], tm_big, 1024, "inproj", BF16)
        z_small = _matmul(hp, c["w_small"], tm_big, N_SMALL, "inproj_small")
        q_r, rows_p, win_p, cmpin, kv_att, v_t = _prep(z_main, cos_p, sin_p, c["gain8"], _tile(mp, (256, 128)),
                                                       True, l, depth, rows_p, win_p)
        r_cmp, c_cmp = _compress_prompt(cmpin, c["wcat"], c["pe_lhs"], bsz, seq)
        nsa_out = _nsa_prompt(q_r, r_cmp, c_cmp, kv_att, v_t, ovt_p, z_small, z_main, bsz, seq, tk_p)
        pool_out = _pool(z_main, jnp.zeros((bsz, 16, 1024), F32), c["pool_w"], pool_scale[l], bsz, seq,
                         _tile(seq, (256, 128)), 0)
        m_out, fin = _ssd(z_main, z_small, jnp.zeros((bsz, 8, CONV_CH), F32),
                          jnp.zeros((bsz, 1024, M_STATE), F32), c["ssd"], bsz, seq, SSD_CHUNK, seq)
        merged = _merge(nsa_out, pool_out, m_out, c["w_branch"], z_main, tm_big, 512)
        xp, hp = _outproj(xp, merged, c["w_out"], g_next, tm_p)
        zm3 = z_main.reshape(bsz, seq, N_MAIN)
        outs_p[0].append(zm3[:, seq - POOL_STATE:, C_POOLU:C_POOLU + 1024].astype(F32))
        outs_p[1].append(_xbc_cols(zm3[:, seq - (CONV_W - 1):]).astype(F32))
        outs_p[2].append(fin.reshape(bsz, M_HEADS, M_HEAD_DIM, M_STATE))
        z_main = _matmul(hs, c["w_main"], ms, 1024, "inproj_s")
        z_small = _matmul(hs, c["w_small"], ms, N_SMALL, "inproj_small_s")
        q_r, rows_s, win_s, _, kv_att = _prep(z_main, cos_s, sin_s, c["gain8"], ms, False, l, depth, rows_s, win_s)
        new_att = jnp.pad(kv_att.reshape(8, dec_b, nq, HEAD_DIM), ((0, 0), (0, 0), (0, 128 - nq), (0, 0)))
        new_att = jnp.where((jnp.arange(128) < dec_t)[None, None, :, None], new_att, jnp.zeros_like(new_att))
        nsa_out = _nsa_decode(page_table, q_r, cache_pages, c["wcat"], c["pe_lhs"], new_att, cache_win_rows, ov_s,
                              z_small, z_main, l, past_len, nq, dec_t, _tile(n_pages, (16, 8, 4, 2)))
        pre16 = jnp.pad(state_pool[l], ((0, 0), (16 - POOL_STATE, 0), (0, 0)))
        pool_out = _pool(z_main, pre16, c["pool_w"], pool_scale[l], dec_b, nq, nq, past_len)
        pre8 = jnp.pad(state_conv[l], ((0, 0), (8 - (CONV_W - 1), 0), (0, 0)))
        m_out, fin = _ssd(z_main, z_small, pre8, state_ssm[l].reshape(dec_b, 1024, M_STATE), c["ssd"], dec_b, nq,
                          nq, dec_t)
        merged = _merge(nsa_out, pool_out, m_out, c["w_branch"], z_main, ms, 1024)
        xs, hs = _outproj(xs, merged, c["w_out"], g_next, ms)
        zm3 = z_main.reshape(dec_b, nq, N_MAIN)
        outs_s[0].append(jnp.concatenate([state_pool[l], zm3[:, :dec_t, C_POOLU:C_POOLU + 1024]],
                                         axis=1)[:, -POOL_STATE:])
        outs_s[1].append(jnp.concatenate([state_conv[l], _xbc_cols(zm3[:, :dec_t])], axis=1)[:, -(CONV_W - 1):])
        outs_s[2].append(fin.reshape(dec_b, M_HEADS, M_HEAD_DIM, M_STATE))

    y_p = xp.reshape(bsz, seq, D_MODEL)
    y_s = xs.reshape(dec_b, nq, D_MODEL)[:, :dec_t]
    wlen = min(WINDOW, seq)
    kv_p = rows_p.reshape(depth, bsz, seq, 4, N_KV_HEADS, HEAD_DIM)
    win_p = win_p.reshape(depth, bsz, seq, 2, N_KV_HEADS, HEAD_DIM)[:, :, seq - wlen:]
    kv_s = rows_s.reshape(depth, dec_b, nq, 4, N_KV_HEADS, HEAD_DIM)[:, :, :dec_t]
    win_new = win_s.reshape(depth, dec_b, nq, 2, N_KV_HEADS, HEAD_DIM)[:, :, :dec_t]
    win_s = jnp.concatenate([cache_win, win_new], axis=2)[:, :, -WINDOW:]
    stk = lambda lst: jnp.stack(lst)
    return (y_p, y_s, kv_p, win_p, *[stk(v) for v in outs_p], kv_s, win_s, *[stk(v) for v in outs_s])
```
